```python
import math
import jax, jax.numpy as jnp
from jax import lax
import numpy as np

D_MODEL = 4096
BATCH = 1
SEQ = 16384
DEPTH = 2

HEAD_DIM = 128
SWA_Q_HEADS = D_MODEL // (2 * HEAD_DIM)
SWA_KV_HEADS = max(SWA_Q_HEADS // 8, 1)
SWA_GROUP = SWA_Q_HEADS // SWA_KV_HEADS
WINDOW = 128
BLOCK = 128
DIFF_HEADS = D_MODEL // (4 * HEAD_DIM)
DIFF_V_DIM = 2 * HEAD_DIM
MIX_WIDTH = SWA_Q_HEADS * HEAD_DIM + DIFF_HEADS * DIFF_V_DIM
N_BIAS_HEADS = SWA_Q_HEADS + DIFF_HEADS
NUM_BUCKETS = 32
MAX_EXACT = NUM_BUCKETS // 2
MAX_DISTANCE = 128
D_FF = ((8 * D_MODEL // 3 + 255) // 256) * 256
CONV_WIDTH = 3
N_MOD = 6
EPS = 1e-6
NEG_INF = -1e30

QA_COLS = SWA_Q_HEADS * HEAD_DIM
KA_COLS = SWA_KV_HEADS * HEAD_DIM
VA_COLS = SWA_KV_HEADS * HEAD_DIM
QB_COLS = DIFF_HEADS * 2 * HEAD_DIM
KB_COLS = DIFF_HEADS * 2 * HEAD_DIM
VB_COLS = DIFF_HEADS * DIFF_V_DIM
IN_COLS = QA_COLS + KA_COLS + VA_COLS + QB_COLS + KB_COLS + VB_COLS
COL_SPLITS = [QA_COLS, QA_COLS + KA_COLS, QA_COLS + KA_COLS + VA_COLS,
              QA_COLS + KA_COLS + VA_COLS + QB_COLS,
              QA_COLS + KA_COLS + VA_COLS + QB_COLS + KB_COLS]

kernel_name = "hymba_swa_sink_diffattn_convffn_adaln"


def rmsnorm(x, g):
    xf = x.astype(jnp.float32)
    y = xf * lax.rsqrt(jnp.mean(xf * xf, axis=-1, keepdims=True) + EPS)
    return (y * g.astype(jnp.float32)).astype(x.dtype)


def modulate(x, g, shift, scale):
    return rmsnorm(x, g) * (1 + scale[:, None, :]) + shift[:, None, :]


def t5_bucket(n):
    n = jnp.maximum(n, 0)
    nf = jnp.maximum(n, 1).astype(jnp.float32)
    large = MAX_EXACT + (jnp.log(nf / MAX_EXACT) / math.log(MAX_DISTANCE / MAX_EXACT)
                         * (NUM_BUCKETS - MAX_EXACT)).astype(jnp.int32)
    large = jnp.minimum(large, NUM_BUCKETS - 1)
    return jnp.where(n < MAX_EXACT, n, large)


def sliding_window_attention(q, k, v, sinks, rel_bias_a):
    B, S = q.shape[0], q.shape[1]
    nb = S // BLOCK
    q = q.reshape(B, nb, BLOCK, SWA_KV_HEADS, SWA_GROUP, HEAD_DIM)

    def band(t):
        t = t.reshape(B, S, SWA_KV_HEADS, HEAD_DIM)
        t = jnp.pad(t, ((0, 0), (BLOCK, 0), (0, 0), (0, 0)))
        t = t.reshape(B, nb + 1, BLOCK, SWA_KV_HEADS, HEAD_DIM)
        return jnp.concatenate([t[:, :-1], t[:, 1:]], axis=2)

    k_band, v_band = band(k), band(v)
    logits = jnp.einsum('bnqhgd,bnkhd->bnhgqk', q, k_band).astype(jnp.float32) * (HEAD_DIM ** -0.5)
    qa = jnp.arange(BLOCK)
    kb = jnp.arange(2 * BLOCK)
    dist = BLOCK + qa[:, None] - kb[None, :]
    bias = rel_bias_a[t5_bucket(dist)].astype(jnp.float32)
    bias = jnp.transpose(bias, (2, 0, 1)).reshape(SWA_KV_HEADS, SWA_GROUP, BLOCK, 2 * BLOCK)
    kpos = (jnp.arange(nb)[:, None] - 1) * BLOCK + kb[None, :]
    valid = ((dist >= 0) & (dist < WINDOW))[None] & (kpos >= 0)[:, None, :]
    logits = jnp.where(valid[None, :, None, None], logits + bias, NEG_INF)
    sink = jnp.broadcast_to(sinks.astype(jnp.float32).reshape(SWA_KV_HEADS, SWA_GROUP, 1, 1),
                            logits.shape[:-1] + (1,))
    probs = jax.nn.softmax(jnp.concatenate([logits, sink], axis=-1), axis=-1)[..., :-1]
    out = jnp.einsum('bnhgqk,bnkhd->bnqhgd', probs.astype(v_band.dtype), v_band)
    return out.reshape(B, S, SWA_Q_HEADS * HEAD_DIM)


def differential_attention(q, k, v, lq1, lk1, lq2, lk2, subln_g, rel_bias_b, lam_init):
    B, S = q.shape[0], q.shape[1]
    nb = S // BLOCK
    q = q.reshape(B, nb, BLOCK, DIFF_HEADS, 2, HEAD_DIM).transpose(1, 0, 2, 3, 4, 5)
    k = k.reshape(B, S, DIFF_HEADS, 2, HEAD_DIM)
    v = v.reshape(B, S, DIFF_HEADS, DIFF_V_DIM)
    lam = (jnp.exp(jnp.sum(lq1.astype(jnp.float32) * lk1.astype(jnp.float32)))
           - jnp.exp(jnp.sum(lq2.astype(jnp.float32) * lk2.astype(jnp.float32))) + lam_init)
    kpos = jnp.arange(S)

    def one_block(args):
        qb, i = args
        logits = jnp.einsum('bqhcd,bkhcd->bhcqk', qb, k).astype(jnp.float32) * (HEAD_DIM ** -0.5)
        n = (i * BLOCK + jnp.arange(BLOCK))[:, None] - kpos[None, :]
        bias = jnp.transpose(rel_bias_b[t5_bucket(n)], (2, 0, 1)).astype(jnp.float32)
        logits = jnp.where((n >= 0)[None, None, None], logits + bias[None, :, None], NEG_INF)
        p = jax.nn.softmax(logits, axis=-1)
        a = p[:, :, 0] - lam * p[:, :, 1]
        return jnp.einsum('bhqk,bkhe->bqhe', a.astype(v.dtype), v)

    out = lax.map(one_block, (q, jnp.arange(nb)))
    out = out.transpose(1, 0, 2, 3, 4).reshape(B, S, DIFF_HEADS, DIFF_V_DIM)
    out = rmsnorm(out, subln_g) * (1 - lam_init)
    return out.reshape(B, S, DIFF_HEADS * DIFF_V_DIM)


def conv_ffn(h, w_up, conv_w, conv_b, w_down):
    S = h.shape[1]
    u = h @ w_up
    gate, up = jnp.split(u, [D_FF], axis=-1)
    gp = jnp.pad(gate, ((0, 0), (CONV_WIDTH - 1, 0), (0, 0)))
    conv = conv_b
    for j in range(CONV_WIDTH):
        conv = conv + gp[:, j:j + S] * conv_w[j]
    return (jax.nn.silu(conv) * up) @ w_down


def setup_inputs(seed: int = 0) -> dict:
    key = jax.random.key(seed)
    ks = jax.random.split(key, 20)
    f = jnp.float32
    nrm = lambda k, shape, s: jax.random.normal(k, shape, f) * s
    return {
        "x": nrm(ks[0], (BATCH, SEQ, D_MODEL), 1.0),
        "c": nrm(ks[1], (BATCH, D_MODEL), 1.0),
        "ada_w": nrm(ks[2], (DEPTH, D_MODEL, N_MOD * D_MODEL), D_MODEL ** -0.5),
        "ada_b": nrm(ks[3], (DEPTH, N_MOD * D_MODEL), 0.02),
        "attn_norm_g": 1.0 + nrm(ks[4], (DEPTH, D_MODEL), 0.02),
        "mlp_norm_g": 1.0 + nrm(ks[5], (DEPTH, D_MODEL), 0.02),
        "w_in": nrm(ks[6], (DEPTH, D_MODEL, IN_COLS), D_MODEL ** -0.5),
        "swa_sinks": nrm(ks[7], (DEPTH, SWA_Q_HEADS), 1.0),
        "diff_lq1": nrm(ks[8], (DEPTH, HEAD_DIM), 0.1),
        "diff_lk1": nrm(ks[9], (DEPTH, HEAD_DIM), 0.1),
        "diff_lq2": nrm(ks[10], (DEPTH, HEAD_DIM), 0.1),
        "diff_lk2": nrm(ks[11], (DEPTH, HEAD_DIM), 0.1),
        "diff_subln_g": 1.0 + nrm(ks[12], (DEPTH, DIFF_V_DIM), 0.02),
        "w_out": nrm(ks[13], (DEPTH, MIX_WIDTH, D_MODEL), MIX_WIDTH ** -0.5),
        "rel_bias": nrm(ks[14], (NUM_BUCKETS, N_BIAS_HEADS), 0.5),
        "w_up": nrm(ks[15], (DEPTH, D_MODEL, 2 * D_FF), D_MODEL ** -0.5),
        "conv_w": nrm(ks[16], (DEPTH, CONV_WIDTH, D_FF), CONV_WIDTH ** -0.5),
        "conv_b": nrm(ks[17], (DEPTH, D_FF), 0.02),
        "w_down": nrm(ks[18], (DEPTH, D_FF, D_MODEL), D_FF ** -0.5),
        "final_g": 1.0 + nrm(ks[19], (D_MODEL,), 0.02),
    }


def reference(x, c, ada_w, ada_b, attn_norm_g, mlp_norm_g, w_in, swa_sinks, diff_lq1, diff_lk1,
              diff_lq2, diff_lk2, diff_subln_g, w_out, rel_bias, w_up, conv_w, conv_b, w_down, final_g):
    rel_a = rel_bias[:, :SWA_Q_HEADS]
    rel_b = rel_bias[:, SWA_Q_HEADS:]
    cs = jax.nn.silu(c)
    for l in range(DEPTH):
        mod = cs @ ada_w[l] + ada_b[l]
        sh_a, sc_a, g_a, sh_m, sc_m, g_m = jnp.split(mod, N_MOD, axis=-1)
        h = modulate(x, attn_norm_g[l], sh_a, sc_a)
        proj = h @ w_in[l]
        qa, ka, va, qb, kb, vb = jnp.split(proj, COL_SPLITS, axis=-1)
        ya = sliding_window_attention(qa, ka, va, swa_sinks[l], rel_a)
        lam_init = 0.8 - 0.6 * math.exp(-0.3 * l)
        yb = differential_attention(qb, kb, vb, diff_lq1[l], diff_lk1[l], diff_lq2[l], diff_lk2[l],
                                    diff_subln_g[l], rel_b, lam_init)
        mix = jnp.concatenate([ya, yb], axis=-1)
        x = x + g_a[:, None, :] * (mix @ w_out[l])
        h = modulate(x, mlp_norm_g[l], sh_m, sc_m)
        x = x + g_m[:, None, :] * conv_ffn(h, w_up[l], conv_w[l], conv_b[l], w_down[l])
    return rmsnorm(x, final_g)
```

```python
import functools
import math

import jax
import jax.numpy as jnp
import numpy as np
from jax import lax
from jax.experimental import pallas as pl
from jax.experimental.pallas import tpu as pltpu

D_MODEL = 4096
SEQ = 16384
DEPTH = 2
HEAD_DIM = 128
SWA_Q_HEADS = 16
SWA_KV_HEADS = 2
SWA_GROUP = 8
WINDOW = 128
BLOCK = 128
DIFF_HEADS = 8
DIFF_V_DIM = 256
NUM_BUCKETS = 32
MAX_EXACT = 16
MAX_DISTANCE = 128
D_FF = 11008
CONV_WIDTH = 3
N_MOD = 6
EPS = 1e-6
NEG_INF = -1e30

QA_COLS = SWA_Q_HEADS * HEAD_DIM
KA_COLS = SWA_KV_HEADS * HEAD_DIM
VA_COLS = SWA_KV_HEADS * HEAD_DIM
QB_COLS = DIFF_HEADS * 2 * HEAD_DIM
KB_COLS = DIFF_HEADS * 2 * HEAD_DIM
VB_COLS = DIFF_HEADS * DIFF_V_DIM
KA_OFF = QA_COLS
VA_OFF = KA_OFF + KA_COLS
QB_OFF = VA_OFF + VA_COLS
KB_OFF = QB_OFF + QB_COLS
VB_OFF = KB_OFF + KB_COLS
IN_COLS = VB_OFF + VB_COLS

BF16 = jnp.bfloat16
F32 = jnp.float32

VMEM_LIMIT_BYTES = 58 * 1024 * 1024

MOD_TN = 512
NORM_TM = 512
PROJ_TM, PROJ_TN = 1024, 512
OUT_TM, OUT_TN = 1024, 512
UP_TM, UP_TN = 1024, 256
DOWN_TM, DOWN_TN = 512, 512
SWA_T = 512
DIFF_T = 512
CARRY_ROWS = 8


def _params(n_axes):
    return pltpu.CompilerParams(
        dimension_semantics=("arbitrary",) * n_axes,
        vmem_limit_bytes=VMEM_LIMIT_BYTES,
    )


def _mod_kernel(c_ref, w_ref, b_ref, o_ref):
    c = c_ref[...]
    cs = c / (1.0 + jnp.exp(-c))
    o_ref[...] = jnp.sum(cs * w_ref[...], axis=0, keepdims=True) + b_ref[...]


def _ada_mod(c, ada_w, ada_b):
    n_out = N_MOD * D_MODEL
    c_col = c.reshape(D_MODEL, 1)
    b3 = ada_b.reshape(DEPTH, 1, n_out)
    return pl.pallas_call(
        _mod_kernel,
        grid=(DEPTH, n_out // MOD_TN),
        in_specs=[
            pl.BlockSpec((D_MODEL, 1), lambda l, j: (0, 0)),
            pl.BlockSpec((None, D_MODEL, MOD_TN), lambda l, j: (l, 0, j)),
            pl.BlockSpec((None, 1, MOD_TN), lambda l, j: (l, 0, j)),
        ],
        out_specs=pl.BlockSpec((None, 1, MOD_TN), lambda l, j: (l, 0, j)),
        out_shape=jax.ShapeDtypeStruct((DEPTH, 1, n_out), F32),
        compiler_params=_params(2),
        name="ada_mod",
    )(c_col, ada_w, b3)


def _modulate_kernel(x_ref, g_ref, sc_ref, sh_ref, o_ref):
    x = x_ref[...]
    y = x * lax.rsqrt(jnp.mean(x * x, axis=-1, keepdims=True) + EPS)
    o_ref[...] = ((y * g_ref[...]) * (1.0 + sc_ref[...]) + sh_ref[...]).astype(o_ref.dtype)


def _modulate(x, g, scale, shift):
    vec = pl.BlockSpec((1, D_MODEL), lambda i: (0, 0))
    return pl.pallas_call(
        _modulate_kernel,
        grid=(SEQ // NORM_TM,),
        in_specs=[pl.BlockSpec((NORM_TM, D_MODEL), lambda i: (i, 0)), vec, vec, vec],
        out_specs=pl.BlockSpec((NORM_TM, D_MODEL), lambda i: (i, 0)),
        out_shape=jax.ShapeDtypeStruct((SEQ, D_MODEL), BF16),
        compiler_params=_params(1),
        name="modulate",
    )(x, g, scale, shift)


def _rmsnorm_kernel(x_ref, g_ref, o_ref):
    x = x_ref[...]
    y = x * lax.rsqrt(jnp.mean(x * x, axis=-1, keepdims=True) + EPS)
    o_ref[...] = y * g_ref[...]


def _final_norm(x, g):
    return pl.pallas_call(
        _rmsnorm_kernel,
        grid=(SEQ // NORM_TM,),
        in_specs=[pl.BlockSpec((NORM_TM, D_MODEL), lambda i: (i, 0)),
                  pl.BlockSpec((1, D_MODEL), lambda i: (0, 0))],
        out_specs=pl.BlockSpec((NORM_TM, D_MODEL), lambda i: (i, 0)),
        out_shape=jax.ShapeDtypeStruct((SEQ, D_MODEL), F32),
        compiler_params=_params(1),
        name="final_norm",
    )(x, g)


def _proj_kernel(h_ref, w_ref, cs_ref, o_ref):
    acc = jnp.dot(h_ref[...], w_ref[...], preferred_element_type=F32)
    o_ref[...] = (acc * cs_ref[...]).astype(o_ref.dtype)


def _in_proj(h, w, col_scale):
    return pl.pallas_call(
        _proj_kernel,
        grid=(SEQ // PROJ_TM, IN_COLS // PROJ_TN),
        in_specs=[
            pl.BlockSpec((PROJ_TM, D_MODEL), lambda i, j: (i, 0)),
            pl.BlockSpec((D_MODEL, PROJ_TN), lambda i, j: (0, j)),
            pl.BlockSpec((1, PROJ_TN), lambda i, j: (0, j)),
        ],
        out_specs=pl.BlockSpec((PROJ_TM, PROJ_TN), lambda i, j: (i, j)),
        out_shape=jax.ShapeDtypeStruct((SEQ, IN_COLS), BF16),
        compiler_params=_params(2),
        name="in_proj",
    )(h, w, col_scale)


def _swa_kernel(sink_ref, q_ref, kh_ref, km_ref, vh_ref, vm_ref, bias_ref, o_ref):
    hkv = pl.program_id(0)
    i = pl.program_id(1)
    k_all = jnp.concatenate([kh_ref[...], km_ref[...]], axis=0)
    v_all = jnp.concatenate([vh_ref[...], vm_ref[...]], axis=0)
    r = lax.broadcasted_iota(jnp.int32, (BLOCK, 2 * BLOCK), 0)
    c = lax.broadcasted_iota(jnp.int32, (BLOCK, 2 * BLOCK), 1)
    dist = BLOCK + r - c
    band_ok = (dist >= 0) & (dist < WINDOW)
    first_ok = band_ok & ((c + jnp.where(i > 0, BLOCK, 0)) >= BLOCK)
    for b in range(SWA_T // BLOCK):
        kb = k_all[b * BLOCK:(b + 2) * BLOCK]
        vb = v_all[b * BLOCK:(b + 2) * BLOCK]
        ok = first_ok if b == 0 else band_ok
        for g in range(SWA_GROUP):
            qg = q_ref[b * BLOCK:(b + 1) * BLOCK, g * HEAD_DIM:(g + 1) * HEAD_DIM]
            s = lax.dot_general(qg, kb, (((1,), (1,)), ((), ())), preferred_element_type=F32)
            s = jnp.where(ok, s + bias_ref[g], NEG_INF)
            sink = sink_ref[hkv * SWA_GROUP + g]
            m = jnp.maximum(jnp.max(s, axis=1, keepdims=True), sink)
            p = jnp.exp(s - m)
            denom = jnp.sum(p, axis=1, keepdims=True) + jnp.exp(sink - m)
            o = jnp.dot(p.astype(BF16), vb, preferred_element_type=F32) / denom
            o_ref[b * BLOCK:(b + 1) * BLOCK, g * HEAD_DIM:(g + 1) * HEAD_DIM] = o.astype(o_ref.dtype)


def _swa(proj, sinks, bias_a):
    t_blocks = SWA_T // BLOCK
    q_w = SWA_GROUP * HEAD_DIM
    k_col = KA_OFF // HEAD_DIM
    v_col = VA_OFF // HEAD_DIM

    def halo(col0):
        return lambda h, i: (jnp.maximum(i * t_blocks - 1, 0), col0 + h)

    return pl.pallas_call(
        _swa_kernel,
        grid=(SWA_KV_HEADS, SEQ // SWA_T),
        in_specs=[
            pl.BlockSpec(memory_space=pltpu.SMEM),
            pl.BlockSpec((SWA_T, q_w), lambda h, i: (i, h)),
            pl.BlockSpec((BLOCK, HEAD_DIM), halo(k_col)),
            pl.BlockSpec((SWA_T, HEAD_DIM), lambda h, i: (i, k_col + h)),
            pl.BlockSpec((BLOCK, HEAD_DIM), halo(v_col)),
            pl.BlockSpec((SWA_T, HEAD_DIM), lambda h, i: (i, v_col + h)),
            pl.BlockSpec((SWA_GROUP, BLOCK, 2 * BLOCK), lambda h, i: (h, 0, 0)),
        ],
        out_specs=pl.BlockSpec((SWA_T, q_w), lambda h, i: (i, h)),
        out_shape=jax.ShapeDtypeStruct((SEQ, QA_COLS), BF16),
        compiler_params=_params(2),
        name="swa_attn",
    )(sinks, proj, proj, proj, proj, proj, bias_a)


def _diff_kernel(q_ref, k_ref, v_ref, bias_ref, lq1_ref, lk1_ref, lq2_ref, lk2_ref, g_ref,
                 o_ref, m_ref, l_ref, acc_ref, *, lam_init):
    t = DIFF_T
    i = pl.program_id(1)
    m_ref[...] = jnp.full(m_ref.shape, -jnp.inf, F32)
    l_ref[...] = jnp.zeros(l_ref.shape, F32)
    acc_ref[...] = jnp.zeros(acc_ref.shape, F32)

    def attend(j, bias_idx):
        start = pl.multiple_of(j * t, t)
        vj = v_ref[pl.ds(start, t), :]
        for c in range(2):
            qc = q_ref[:, c * HEAD_DIM:(c + 1) * HEAD_DIM]
            kc = k_ref[pl.ds(start, t), pl.ds(c * HEAD_DIM, HEAD_DIM)]
            s = lax.dot_general(qc, kc, (((1,), (1,)), ((), ())), preferred_element_type=F32)
            if bias_idx is not None:
                s = s + bias_ref[bias_idx]
            if bias_idx == 1:
                row = lax.broadcasted_iota(jnp.int32, (t, t), 0)
                col = lax.broadcasted_iota(jnp.int32, (t, t), 1)
                s = jnp.where(row >= col, s, NEG_INF)
            m_prev = m_ref[c]
            m_new = jnp.maximum(m_prev, jnp.max(s, axis=1, keepdims=True))
            alpha = jnp.exp(m_prev - m_new)
            p = jnp.exp(s - m_new)
            l_ref[c] = alpha * l_ref[c] + jnp.sum(p, axis=1, keepdims=True)
            acc_ref[c] = alpha * acc_ref[c] + jnp.dot(p.astype(BF16), vj, preferred_element_type=F32)
            m_ref[c] = m_new

    def far_body(j, carry):
        attend(j, None)
        return carry

    lax.fori_loop(0, jnp.maximum(i - 1, 0), far_body, 0)

    @pl.when(i >= 1)
    def _():
        attend(i - 1, 0)

    attend(i, 1)

    lam = (jnp.exp(jnp.sum(lq1_ref[...] * lk1_ref[...], axis=1, keepdims=True))
           - jnp.exp(jnp.sum(lq2_ref[...] * lk2_ref[...], axis=1, keepdims=True)) + lam_init)
    out = acc_ref[0] / l_ref[0] - lam * (acc_ref[1] / l_ref[1])
    y = out * lax.rsqrt(jnp.mean(out * out, axis=-1, keepdims=True) + EPS)
    o_ref[...] = ((y * g_ref[...]) * (1.0 - lam_init)).astype(o_ref.dtype)


def _diff_attn(proj, bias_b, lq1, lk1, lq2, lk2, subln_g, lam_init):
    t = DIFF_T
    w = 2 * HEAD_DIM
    q_col, k_col, v_col = QB_OFF // w, KB_OFF // w, VB_OFF // w
    vec = pl.BlockSpec((1, HEAD_DIM), lambda h, i: (0, 0))
    return pl.pallas_call(
        functools.partial(_diff_kernel, lam_init=lam_init),
        grid=(DIFF_HEADS, SEQ // t),
        in_specs=[
            pl.BlockSpec((t, w), lambda h, i: (i, q_col + h)),
            pl.BlockSpec((SEQ, w), lambda h, i: (0, k_col + h)),
            pl.BlockSpec((SEQ, w), lambda h, i: (0, v_col + h)),
            pl.BlockSpec((None, 2, t, t), lambda h, i: (h, 0, 0, 0)),
            vec, vec, vec, vec,
            pl.BlockSpec((1, DIFF_V_DIM), lambda h, i: (0, 0)),
        ],
        out_specs=pl.BlockSpec((t, DIFF_V_DIM), lambda h, i: (i, h)),
        out_shape=jax.ShapeDtypeStruct((SEQ, VB_COLS), BF16),
        scratch_shapes=[
            pltpu.VMEM((2, t, 1), F32),
            pltpu.VMEM((2, t, 1), F32),
            pltpu.VMEM((2, t, DIFF_V_DIM), F32),
        ],
        compiler_params=_params(2),
        name="diff_attn",
    )(proj, proj, proj, bias_b, lq1, lk1, lq2, lk2, subln_g)


def _out_kernel(ya_ref, yb_ref, wa_ref, wb_ref, x_ref, g_ref, o_ref):
    acc = jnp.dot(ya_ref[...], wa_ref[...], preferred_element_type=F32)
    acc = acc + jnp.dot(yb_ref[...], wb_ref[...], preferred_element_type=F32)
    o_ref[...] = x_ref[...] + g_ref[...] * acc


def _out_proj(ya, yb, w, x, gate):
    return pl.pallas_call(
        _out_kernel,
        grid=(SEQ // OUT_TM, D_MODEL // OUT_TN),
        in_specs=[
            pl.BlockSpec((OUT_TM, QA_COLS), lambda i, j: (i, 0)),
            pl.BlockSpec((OUT_TM, VB_COLS), lambda i, j: (i, 0)),
            pl.BlockSpec((QA_COLS, OUT_TN), lambda i, j: (0, j)),
            pl.BlockSpec((VB_COLS, OUT_TN), lambda i, j: (1, j)),
            pl.BlockSpec((OUT_TM, OUT_TN), lambda i, j: (i, j)),
            pl.BlockSpec((1, OUT_TN), lambda i, j: (0, j)),
        ],
        out_specs=pl.BlockSpec((OUT_TM, OUT_TN), lambda i, j: (i, j)),
        out_shape=jax.ShapeDtypeStruct((SEQ, D_MODEL), F32),
        compiler_params=_params(2),
        name="out_proj",
    )(ya, yb, w, w, x, gate)


def _up_kernel(h_ref, wg_ref, wu_ref, cw_ref, cb_ref, o_ref, gs_ref):
    tm = UP_TM

    @pl.when(pl.program_id(1) == 0)
    def _():
        gs_ref[0:CARRY_ROWS, :] = jnp.zeros((CARRY_ROWS, UP_TN), F32)

    h = h_ref[...]
    gs_ref[CARRY_ROWS:CARRY_ROWS + tm, :] = jnp.dot(h, wg_ref[...], preferred_element_type=F32)
    up = jnp.dot(h, wu_ref[...], preferred_element_type=F32)
    conv = cb_ref[...]
    for j in range(CONV_WIDTH):
        off = CARRY_ROWS - (CONV_WIDTH - 1) + j
        conv = conv + gs_ref[off:off + tm, :] * cw_ref[j:j + 1, :]
    act = conv / (1.0 + jnp.exp(-conv)) * up
    o_ref[...] = act.astype(o_ref.dtype)
    gs_ref[0:CARRY_ROWS, :] = gs_ref[tm:tm + CARRY_ROWS, :]


def _mlp_up(h, w_up, conv_w, conv_b):
    n_blocks = D_FF // UP_TN
    return pl.pallas_call(
        _up_kernel,
        grid=(n_blocks, SEQ // UP_TM),
        in_specs=[
            pl.BlockSpec((UP_TM, D_MODEL), lambda j, i: (i, 0)),
            pl.BlockSpec((D_MODEL, UP_TN), lambda j, i: (0, j)),
            pl.BlockSpec((D_MODEL, UP_TN), lambda j, i: (0, n_blocks + j)),
            pl.BlockSpec((CONV_WIDTH, UP_TN), lambda j, i: (0, j)),
            pl.BlockSpec((1, UP_TN), lambda j, i: (0, j)),
        ],
        out_specs=pl.BlockSpec((UP_TM, UP_TN), lambda j, i: (i, j)),
        out_shape=jax.ShapeDtypeStruct((SEQ, D_FF), BF16),
        scratch_shapes=[pltpu.VMEM((UP_TM + CARRY_ROWS, UP_TN), F32)],
        compiler_params=_params(2),
        name="mlp_up",
    )(h, w_up, w_up, conv_w, conv_b)


def _down_kernel(a_ref, w_ref, x_ref, g_ref, o_ref):
    acc = jnp.dot(a_ref[...], w_ref[...], preferred_element_type=F32)
    o_ref[...] = x_ref[...] + g_ref[...] * acc


def _mlp_down(act, w, x, gate):
    return pl.pallas_call(
        _down_kernel,
        grid=(D_MODEL // DOWN_TN, SEQ // DOWN_TM),
        in_specs=[
            pl.BlockSpec((DOWN_TM, D_FF), lambda j, i: (i, 0)),
            pl.BlockSpec((D_FF, DOWN_TN), lambda j, i: (0, j)),
            pl.BlockSpec((DOWN_TM, DOWN_TN), lambda j, i: (i, j)),
            pl.BlockSpec((1, DOWN_TN), lambda j, i: (0, j)),
        ],
        out_specs=pl.BlockSpec((DOWN_TM, DOWN_TN), lambda j, i: (i, j)),
        out_shape=jax.ShapeDtypeStruct((SEQ, D_MODEL), F32),
        compiler_params=_params(2),
        name="mlp_down",
    )(act, w, x, gate)


def _t5_bucket(n):
    n = jnp.maximum(n, 0)
    nf = jnp.maximum(n, 1).astype(F32)
    large = MAX_EXACT + (jnp.log(nf / MAX_EXACT) / math.log(MAX_DISTANCE / MAX_EXACT)
                         * (NUM_BUCKETS - MAX_EXACT)).astype(jnp.int32)
    large = jnp.minimum(large, NUM_BUCKETS - 1)
    return jnp.where(n < MAX_EXACT, n, large)


def _bias_tables(rel_bias):
    rel_a = rel_bias[:, :SWA_Q_HEADS]
    rel_b = rel_bias[:, SWA_Q_HEADS:]
    qa = jnp.arange(BLOCK)
    kb = jnp.arange(2 * BLOCK)
    dist_a = BLOCK + qa[:, None] - kb[None, :]
    bias_a = jnp.transpose(rel_a[_t5_bucket(dist_a)], (2, 0, 1)).astype(F32)
    t = DIFF_T
    r = jnp.arange(t)
    dist_prev = t + r[:, None] - r[None, :]
    dist_diag = r[:, None] - r[None, :]
    dist_b = jnp.stack([dist_prev, dist_diag])
    far = rel_b[NUM_BUCKETS - 1]
    bias_b = jnp.transpose(rel_b[_t5_bucket(dist_b)] - far, (3, 0, 1, 2)).astype(F32)
    return bias_a, bias_b


def _col_scale():
    s = np.ones((1, IN_COLS), np.float32)
    s[:, :QA_COLS] = HEAD_DIM ** -0.5
    s[:, QB_OFF:QB_OFF + QB_COLS] = HEAD_DIM ** -0.5
    return jnp.asarray(s)


def kernel(x, c, ada_w, ada_b, attn_norm_g, mlp_norm_g, w_in, swa_sinks, diff_lq1, diff_lk1,
           diff_lq2, diff_lk2, diff_subln_g, w_out, rel_bias, w_up, conv_w, conv_b, w_down, final_g):
    assert x.shape == (1, SEQ, D_MODEL) and c.shape == (1, D_MODEL)
    assert DIFF_T >= MAX_DISTANCE
    xs = x.reshape(SEQ, D_MODEL)
    mod = _ada_mod(c, ada_w, ada_b)
    bias_a, bias_b = _bias_tables(rel_bias)
    col_scale = _col_scale()
    for l in range(DEPTH):
        sh_a, sc_a, g_a, sh_m, sc_m, g_m = [mod[l, :, k * D_MODEL:(k + 1) * D_MODEL] for k in range(N_MOD)]
        lam_init = 0.8 - 0.6 * math.exp(-0.3 * l)
        h = _modulate(xs, attn_norm_g[l][None], sc_a, sh_a)
        proj = _in_proj(h, w_in[l].astype(BF16), col_scale)
        ya = _swa(proj, swa_sinks[l], bias_a)
        yb = _diff_attn(proj, bias_b, diff_lq1[l][None], diff_lk1[l][None], diff_lq2[l][None],
                        diff_lk2[l][None], diff_subln_g[l][None], lam_init)
        xs = _out_proj(ya, yb, w_out[l].astype(BF16), xs, g_a)
        h = _modulate(xs, mlp_norm_g[l][None], sc_m, sh_m)
        act = _mlp_up(h, w_up[l].astype(BF16), conv_w[l], conv_b[l][None])
        xs = _mlp_down(act, w_down[l].astype(BF16), xs, g_m)
    return _final_norm(xs, final_g[None]).reshape(1, SEQ, D_MODEL)
```

```python
import functools
import math

import jax
import jax.numpy as jnp
import numpy as np
from jax import lax
from jax.experimental import pallas as pl
from jax.experimental.pallas import tpu as pltpu

D_MODEL = 4096
SEQ = 16384
DEPTH = 2
HEAD_DIM = 128
SWA_Q_HEADS = 16
SWA_KV_HEADS = 2
SWA_GROUP = 8
WINDOW = 128
BLOCK = 128
DIFF_HEADS = 8
DIFF_V_DIM = 256
NUM_BUCKETS = 32
MAX_EXACT = 16
MAX_DISTANCE = 128
D_FF = 11008
CONV_WIDTH = 3
N_MOD = 6
EPS = 1e-6
NEG_INF = -1e30
LOG2E = math.log2(math.e)

QA_COLS = SWA_Q_HEADS * HEAD_DIM
KA_COLS = SWA_KV_HEADS * HEAD_DIM
VA_COLS = SWA_KV_HEADS * HEAD_DIM
QB_COLS = DIFF_HEADS * 2 * HEAD_DIM
KB_COLS = DIFF_HEADS * 2 * HEAD_DIM
VB_COLS = DIFF_HEADS * DIFF_V_DIM
KA_OFF = QA_COLS
VA_OFF = KA_OFF + KA_COLS
QB_OFF = VA_OFF + VA_COLS
KB_OFF = QB_OFF + QB_COLS
VB_OFF = KB_OFF + KB_COLS
IN_COLS = VB_OFF + VB_COLS

BF16 = jnp.bfloat16
F32 = jnp.float32

VMEM_LIMIT_BYTES = 58 * 1024 * 1024

MOD_TN = 512
NORM_TM = 512
PROJ_TM, PROJ_TN = 1024, 512
OUT_TM, OUT_TN = 1024, 512
UP_TM, UP_TN = 1024, 256
UP_CHUNK = 512
DOWN_TM, DOWN_TN = 512, 512
SWA_T = 512
DIFF_T = 512
CARRY_ROWS = 8


def _params(n_axes):
    return pltpu.CompilerParams(
        dimension_semantics=("arbitrary",) * n_axes,
        vmem_limit_bytes=VMEM_LIMIT_BYTES,
    )


def _mod_kernel(c_ref, w_ref, b_ref, o_ref):
    c = c_ref[...]
    cs = c / (1.0 + jnp.exp(-c))
    o_ref[...] = jnp.sum(cs * w_ref[...], axis=0, keepdims=True) + b_ref[...]


def _ada_mod(c, ada_w, ada_b):
    n_out = N_MOD * D_MODEL
    c_col = c.reshape(D_MODEL, 1)
    b3 = ada_b.reshape(DEPTH, 1, n_out)
    return pl.pallas_call(
        _mod_kernel,
        grid=(DEPTH, n_out // MOD_TN),
        in_specs=[
            pl.BlockSpec((D_MODEL, 1), lambda l, j: (0, 0)),
            pl.BlockSpec((None, D_MODEL, MOD_TN), lambda l, j: (l, 0, j)),
            pl.BlockSpec((None, 1, MOD_TN), lambda l, j: (l, 0, j)),
        ],
        out_specs=pl.BlockSpec((None, 1, MOD_TN), lambda l, j: (l, 0, j)),
        out_shape=jax.ShapeDtypeStruct((DEPTH, 1, n_out), F32),
        compiler_params=_params(2),
        name="ada_mod",
    )(c_col, ada_w, b3)


def _modulate_kernel(x_ref, g_ref, sc_ref, sh_ref, o_ref):
    x = x_ref[...]
    y = x * lax.rsqrt(jnp.mean(x * x, axis=-1, keepdims=True) + EPS)
    o_ref[...] = ((y * g_ref[...]) * (1.0 + sc_ref[...]) + sh_ref[...]).astype(o_ref.dtype)


def _modulate(x, g, scale, shift):
    vec = pl.BlockSpec((1, D_MODEL), lambda i: (0, 0))
    return pl.pallas_call(
        _modulate_kernel,
        grid=(SEQ // NORM_TM,),
        in_specs=[pl.BlockSpec((NORM_TM, D_MODEL), lambda i: (i, 0)), vec, vec, vec],
        out_specs=pl.BlockSpec((NORM_TM, D_MODEL), lambda i: (i, 0)),
        out_shape=jax.ShapeDtypeStruct((SEQ, D_MODEL), BF16),
        compiler_params=_params(1),
        name="modulate",
    )(x, g, scale, shift)


def _rmsnorm_kernel(x_ref, g_ref, o_ref):
    x = x_ref[...]
    y = x * lax.rsqrt(jnp.mean(x * x, axis=-1, keepdims=True) + EPS)
    o_ref[...] = y * g_ref[...]


def _final_norm(x, g):
    return pl.pallas_call(
        _rmsnorm_kernel,
        grid=(SEQ // NORM_TM,),
        in_specs=[pl.BlockSpec((NORM_TM, D_MODEL), lambda i: (i, 0)),
                  pl.BlockSpec((1, D_MODEL), lambda i: (0, 0))],
        out_specs=pl.BlockSpec((NORM_TM, D_MODEL), lambda i: (i, 0)),
        out_shape=jax.ShapeDtypeStruct((SEQ, D_MODEL), F32),
        compiler_params=_params(1),
        name="final_norm",
    )(x, g)


def _proj_kernel(h_ref, w_ref, cs_ref, o_ref):
    acc = jnp.dot(h_ref[...], w_ref[...], preferred_element_type=F32)
    o_ref[...] = (acc * cs_ref[...]).astype(o_ref.dtype)


def _in_proj(h, w, col_scale):
    return pl.pallas_call(
        _proj_kernel,
        grid=(SEQ // PROJ_TM, IN_COLS // PROJ_TN),
        in_specs=[
            pl.BlockSpec((PROJ_TM, D_MODEL), lambda i, j: (i, 0)),
            pl.BlockSpec((D_MODEL, PROJ_TN), lambda i, j: (0, j)),
            pl.BlockSpec((1, PROJ_TN), lambda i, j: (0, j)),
        ],
        out_specs=pl.BlockSpec((PROJ_TM, PROJ_TN), lambda i, j: (i, j)),
        out_shape=jax.ShapeDtypeStruct((SEQ, IN_COLS), BF16),
        compiler_params=_params(2),
        name="in_proj",
    )(h, w, col_scale)


def _swa_kernel(sink_ref, q_ref, kh_ref, km_ref, vh_ref, vm_ref, bias_ref, o_ref):
    hkv = pl.program_id(0)
    i = pl.program_id(1)
    k_all = jnp.concatenate([kh_ref[...], km_ref[...]], axis=0)
    v_all = jnp.concatenate([vh_ref[...], vm_ref[...]], axis=0)
    r = lax.broadcasted_iota(jnp.int32, (BLOCK, 2 * BLOCK), 0)
    c = lax.broadcasted_iota(jnp.int32, (BLOCK, 2 * BLOCK), 1)
    dist = BLOCK + r - c
    band_ok = (dist >= 0) & (dist < WINDOW)
    first_ok = band_ok & ((c + jnp.where(i > 0, BLOCK, 0)) >= BLOCK)
    for b in range(SWA_T // BLOCK):
        kb = k_all[b * BLOCK:(b + 2) * BLOCK]
        vb = v_all[b * BLOCK:(b + 2) * BLOCK]
        ok = first_ok if b == 0 else band_ok
        for g in range(SWA_GROUP):
            qg = q_ref[b * BLOCK:(b + 1) * BLOCK, g * HEAD_DIM:(g + 1) * HEAD_DIM]
            s = lax.dot_general(qg, kb, (((1,), (1,)), ((), ())), preferred_element_type=F32)
            s = jnp.where(ok, s + bias_ref[g], NEG_INF)
            sink = sink_ref[hkv * SWA_GROUP + g]
            m = jnp.maximum(jnp.max(s, axis=1, keepdims=True), sink)
            p = jnp.exp(s - m)
            denom = jnp.sum(p, axis=1, keepdims=True) + jnp.exp(sink - m)
            o = jnp.dot(p.astype(BF16), vb, preferred_element_type=F32) / denom
            o_ref[b * BLOCK:(b + 1) * BLOCK, g * HEAD_DIM:(g + 1) * HEAD_DIM] = o.astype(o_ref.dtype)


def _swa(proj, sinks, bias_a):
    t_blocks = SWA_T // BLOCK
    q_w = SWA_GROUP * HEAD_DIM
    k_col = KA_OFF // HEAD_DIM
    v_col = VA_OFF // HEAD_DIM

    def halo(col0):
        return lambda h, i: (jnp.maximum(i * t_blocks - 1, 0), col0 + h)

    return pl.pallas_call(
        _swa_kernel,
        grid=(SWA_KV_HEADS, SEQ // SWA_T),
        in_specs=[
            pl.BlockSpec(memory_space=pltpu.SMEM),
            pl.BlockSpec((SWA_T, q_w), lambda h, i: (i, h)),
            pl.BlockSpec((BLOCK, HEAD_DIM), halo(k_col)),
            pl.BlockSpec((SWA_T, HEAD_DIM), lambda h, i: (i, k_col + h)),
            pl.BlockSpec((BLOCK, HEAD_DIM), halo(v_col)),
            pl.BlockSpec((SWA_T, HEAD_DIM), lambda h, i: (i, v_col + h)),
            pl.BlockSpec((SWA_GROUP, BLOCK, 2 * BLOCK), lambda h, i: (h, 0, 0)),
        ],
        out_specs=pl.BlockSpec((SWA_T, q_w), lambda h, i: (i, h)),
        out_shape=jax.ShapeDtypeStruct((SEQ, QA_COLS), BF16),
        compiler_params=_params(2),
        name="swa_attn",
    )(sinks, proj, proj, proj, proj, proj, bias_a)


def _diff_kernel(q_ref, k_ref, v_ref, bdiag_ref, bcorner_ref, lq1_ref, lk1_ref, lq2_ref, lk2_ref, g_ref,
                 o_ref, m_ref, l_ref, acc_ref, s_buf, mx_buf, p_buf, a_buf, *, lam_init):
    t = DIFF_T
    n_lane = t // BLOCK
    i = pl.program_id(1)
    n_far = jnp.maximum(i - 1, 0)
    n_pairs = n_far // 2

    m_ref[...] = jnp.full(m_ref.shape, NEG_INF, F32)
    l_ref[...] = jnp.zeros(l_ref.shape, F32)
    acc_ref[...] = jnp.zeros(acc_ref.shape, F32)
    s_buf[1] = jnp.full(s_buf.shape[1:], -jnp.inf, F32)
    mx_buf[1] = jnp.full(mx_buf.shape[1:], -jnp.inf, F32)
    p_buf[0] = jnp.zeros(p_buf.shape[1:], BF16)
    a_buf[0] = jnp.ones(a_buf.shape[1:], F32)

    def lane_tiles(x):
        return [x[:, n * BLOCK:(n + 1) * BLOCK] for n in range(x.shape[1] // BLOCK)]

    def stage_a(slot, j, kind, kill=None, corner=False):
        start = pl.multiple_of(j * t, t)
        for c in range(2):
            qc = q_ref[:, c * HEAD_DIM:(c + 1) * HEAD_DIM]
            kc = k_ref[pl.ds(start, t), pl.ds(c * HEAD_DIM, HEAD_DIM)]
            s = lax.dot_general(qc, kc, (((1,), (1,)), ((), ())), preferred_element_type=F32)
            if kind == "tail":
                s = s + kill
                if corner:
                    near = s[:BLOCK, t - BLOCK:] + bcorner_ref[...]
                    top = jnp.concatenate([s[:BLOCK, :t - BLOCK], near], axis=1)
                    s = jnp.concatenate([top, s[BLOCK:]], axis=0)
            elif kind == "diag":
                row = lax.broadcasted_iota(jnp.int32, (t, t), 0)
                col = lax.broadcasted_iota(jnp.int32, (t, t), 1)
                s = jnp.where(row >= col, s + bdiag_ref[...], NEG_INF)
            s_buf[slot, c] = s
            mx_buf[slot, c] = functools.reduce(jnp.maximum, lane_tiles(s))

    def stage_b(slot):
        for c in range(2):
            m_prev = m_ref[c]
            m_new = jnp.maximum(m_prev, jnp.max(mx_buf[slot, c], axis=1, keepdims=True))
            alpha = jnp.exp2(m_prev - m_new)
            p = jnp.exp2(s_buf[slot, c] - jnp.concatenate([m_new] * n_lane, axis=1))
            l_ref[c] = alpha * l_ref[c] + functools.reduce(jnp.add, lane_tiles(p))
            p_buf[slot, c] = p.astype(BF16)
            a_buf[slot, c] = alpha
            m_ref[c] = m_new

    def stage_c(slot, j):
        start = pl.multiple_of(j * t, t)
        vj = v_ref[pl.ds(start, t), :]
        for c in range(2):
            alpha = a_buf[slot, c]
            pv = jnp.dot(p_buf[slot, c], vj, preferred_element_type=F32)
            acc_ref[c] = jnp.concatenate([alpha] * (DIFF_V_DIM // BLOCK), axis=1) * acc_ref[c] + pv

    def pair(u, carry):
        t0 = 2 * u
        stage_c(0, jnp.maximum(t0 - 2, 0))
        stage_b(1)
        stage_a(0, t0, "far")
        stage_c(1, jnp.maximum(t0 - 1, 0))
        stage_b(0)
        stage_a(1, t0 + 1, "far")
        return carry

    lax.fori_loop(0, n_pairs, pair, 0)

    e = 2 * n_pairs
    kill_e = jnp.where(n_far % 2 == 1, 0.0, -jnp.inf).astype(F32)
    prev = jnp.maximum(i - 1, 0)
    kill_p = jnp.where(i >= 1, 0.0, -jnp.inf).astype(F32)
    stage_c(0, jnp.maximum(e - 2, 0))
    stage_b(1)
    stage_a(0, e, "tail", kill_e)
    stage_c(1, jnp.maximum(e - 1, 0))
    stage_b(0)
    stage_a(1, prev, "tail", kill_p, corner=True)
    stage_c(0, e)
    stage_b(1)
    stage_a(0, i, "diag")
    stage_c(1, prev)
    stage_b(0)
    stage_c(0, i)

    lam = (jnp.exp(jnp.sum(lq1_ref[...] * lk1_ref[...], axis=1, keepdims=True))
           - jnp.exp(jnp.sum(lq2_ref[...] * lk2_ref[...], axis=1, keepdims=True)) + lam_init)
    l1 = jnp.sum(l_ref[0], axis=1, keepdims=True)
    l2 = jnp.sum(l_ref[1], axis=1, keepdims=True)
    out = acc_ref[0] / l1 - lam * (acc_ref[1] / l2)
    y = out * lax.rsqrt(jnp.mean(out * out, axis=-1, keepdims=True) + EPS)
    o_ref[...] = ((y * g_ref[...]) * (1.0 - lam_init)).astype(o_ref.dtype)


def _diff_attn(proj, bias_diag, bias_corner, lq1, lk1, lq2, lk2, subln_g, lam_init):
    t = DIFF_T
    w = 2 * HEAD_DIM
    q_col, k_col, v_col = QB_OFF // w, KB_OFF // w, VB_OFF // w
    vec = pl.BlockSpec((1, HEAD_DIM), lambda h, i: (0, 0))
    return pl.pallas_call(
        functools.partial(_diff_kernel, lam_init=lam_init),
        grid=(DIFF_HEADS, SEQ // t),
        in_specs=[
            pl.BlockSpec((t, w), lambda h, i: (i, q_col + h)),
            pl.BlockSpec((SEQ, w), lambda h, i: (0, k_col + h)),
            pl.BlockSpec((SEQ, w), lambda h, i: (0, v_col + h)),
            pl.BlockSpec((None, t, t), lambda h, i: (h, 0, 0)),
            pl.BlockSpec((None, BLOCK, BLOCK), lambda h, i: (h, 0, 0)),
            vec, vec, vec, vec,
            pl.BlockSpec((1, DIFF_V_DIM), lambda h, i: (0, 0)),
        ],
        out_specs=pl.BlockSpec((t, DIFF_V_DIM), lambda h, i: (i, h)),
        out_shape=jax.ShapeDtypeStruct((SEQ, VB_COLS), BF16),
        scratch_shapes=[
            pltpu.VMEM((2, t, BLOCK), F32),
            pltpu.VMEM((2, t, BLOCK), F32),
            pltpu.VMEM((2, t, DIFF_V_DIM), F32),
            pltpu.VMEM((2, 2, t, t), F32),
            pltpu.VMEM((2, 2, t, BLOCK), F32),
            pltpu.VMEM((2, 2, t, t), BF16),
            pltpu.VMEM((2, 2, t, BLOCK), F32),
        ],
        compiler_params=_params(2),
        name="diff_attn",
    )(proj, proj, proj, bias_diag, bias_corner, lq1, lk1, lq2, lk2, subln_g)


def _out_kernel(ya_ref, yb_ref, wa_ref, wb_ref, x_ref, g_ref, o_ref):
    acc = jnp.dot(ya_ref[...], wa_ref[...], preferred_element_type=F32)
    acc = acc + jnp.dot(yb_ref[...], wb_ref[...], preferred_element_type=F32)
    o_ref[...] = x_ref[...] + g_ref[...] * acc


def _out_proj(ya, yb, w, x, gate):
    return pl.pallas_call(
        _out_kernel,
        grid=(SEQ // OUT_TM, D_MODEL // OUT_TN),
        in_specs=[
            pl.BlockSpec((OUT_TM, QA_COLS), lambda i, j: (i, 0)),
            pl.BlockSpec((OUT_TM, VB_COLS), lambda i, j: (i, 0)),
            pl.BlockSpec((QA_COLS, OUT_TN), lambda i, j: (0, j)),
            pl.BlockSpec((VB_COLS, OUT_TN), lambda i, j: (1, j)),
            pl.BlockSpec((OUT_TM, OUT_TN), lambda i, j: (i, j)),
            pl.BlockSpec((1, OUT_TN), lambda i, j: (0, j)),
        ],
        out_specs=pl.BlockSpec((OUT_TM, OUT_TN), lambda i, j: (i, j)),
        out_shape=jax.ShapeDtypeStruct((SEQ, D_MODEL), F32),
        compiler_params=_params(2),
        name="out_proj",
    )(ya, yb, w, w, x, gate)


def _up_kernel(h_ref, wg_ref, wu_ref, cw_ref, cb_ref, o_ref, carry_ref):
    @pl.when(pl.program_id(1) == 0)
    def _():
        carry_ref[...] = jnp.zeros(carry_ref.shape, F32)

    prev = carry_ref[...]
    row = lax.broadcasted_iota(jnp.int32, (CARRY_ROWS, UP_TN), 0)
    for r0 in range(0, UP_TM, UP_CHUNK):
        h = h_ref[r0:r0 + UP_CHUNK, :]
        gate = jnp.dot(h, wg_ref[...], preferred_element_type=F32)
        up = jnp.dot(h, wu_ref[...], preferred_element_type=F32)
        conv = cb_ref[...] + gate * cw_ref[CONV_WIDTH - 1:CONV_WIDTH, :]
        for d in range(1, CONV_WIDTH):
            shifted = pltpu.roll(gate, d, 0)
            head = jnp.where(row < d, pltpu.roll(prev, d, 0), shifted[:CARRY_ROWS])
            shifted = jnp.concatenate([head, shifted[CARRY_ROWS:]], axis=0)
            conv = conv + shifted * cw_ref[CONV_WIDTH - 1 - d:CONV_WIDTH - d, :]
        act = conv / (1.0 + jnp.exp(-conv)) * up
        o_ref[r0:r0 + UP_CHUNK, :] = act.astype(o_ref.dtype)
        prev = gate[UP_CHUNK - CARRY_ROWS:]
    carry_ref[...] = prev


def _mlp_up(h, w_up, conv_w, conv_b):
    n_blocks = D_FF // UP_TN
    return pl.pallas_call(
        _up_kernel,
        grid=(n_blocks, SEQ // UP_TM),
        in_specs=[
            pl.BlockSpec((UP_TM, D_MODEL), lambda j, i: (i, 0)),
            pl.BlockSpec((D_MODEL, UP_TN), lambda j, i: (0, j)),
            pl.BlockSpec((D_MODEL, UP_TN), lambda j, i: (0, n_blocks + j)),
            pl.BlockSpec((CONV_WIDTH, UP_TN), lambda j, i: (0, j)),
            pl.BlockSpec((1, UP_TN), lambda j, i: (0, j)),
        ],
        out_specs=pl.BlockSpec((UP_TM, UP_TN), lambda j, i: (i, j)),
        out_shape=jax.ShapeDtypeStruct((SEQ, D_FF), BF16),
        scratch_shapes=[pltpu.VMEM((CARRY_ROWS, UP_TN), F32)],
        compiler_params=_params(2),
        name="mlp_up",
    )(h, w_up, w_up, conv_w, conv_b)


def _down_kernel(a_ref, w_ref, x_ref, g_ref, o_ref):
    acc = jnp.dot(a_ref[...], w_ref[...], preferred_element_type=F32)
    o_ref[...] = x_ref[...] + g_ref[...] * acc


def _mlp_down(act, w, x, gate):
    return pl.pallas_call(
        _down_kernel,
        grid=(D_MODEL // DOWN_TN, SEQ // DOWN_TM),
        in_specs=[
            pl.BlockSpec((DOWN_TM, D_FF), lambda j, i: (i, 0)),
            pl.BlockSpec((D_FF, DOWN_TN), lambda j, i: (0, j)),
            pl.BlockSpec((DOWN_TM, DOWN_TN), lambda j, i: (i, j)),
            pl.BlockSpec((1, DOWN_TN), lambda j, i: (0, j)),
        ],
        out_specs=pl.BlockSpec((DOWN_TM, DOWN_TN), lambda j, i: (i, j)),
        out_shape=jax.ShapeDtypeStruct((SEQ, D_MODEL), F32),
        compiler_params=_params(2),
        name="mlp_down",
    )(act, w, x, gate)


def _t5_bucket(n):
    n = jnp.maximum(n, 0)
    nf = jnp.maximum(n, 1).astype(F32)
    large = MAX_EXACT + (jnp.log(nf / MAX_EXACT) / math.log(MAX_DISTANCE / MAX_EXACT)
                         * (NUM_BUCKETS - MAX_EXACT)).astype(jnp.int32)
    large = jnp.minimum(large, NUM_BUCKETS - 1)
    return jnp.where(n < MAX_EXACT, n, large)


def _toeplitz_tiles(rel):
    n_heads = rel.shape[1]
    period = 3 * BLOCK
    u = np.arange(period)
    by_dist = rel[_t5_bucket(jnp.asarray(u - (BLOCK - 1)))]
    w = by_dist[(2 * BLOCK - 1 - u) % period].T
    skew = jnp.tile(w, (1, BLOCK))[:, :BLOCK * (period - 1)].reshape(n_heads, BLOCK, period - 1)
    return skew[:, :, :2 * BLOCK]


def _bias_tables(rel_bias):
    tiles = _toeplitz_tiles(rel_bias.astype(F32))
    bias_a = tiles[:SWA_Q_HEADS]
    far = rel_bias[NUM_BUCKETS - 1, SWA_Q_HEADS:].astype(F32)
    tb = (tiles[SWA_Q_HEADS:] - far[:, None, None]) * LOG2E
    sub_diag, on_diag = tb[:, :, :BLOCK], tb[:, :, BLOCK:]
    n = DIFF_T // BLOCK
    zero = jnp.zeros_like(on_diag)
    rows = [jnp.concatenate([on_diag if r == c else sub_diag if r == c + 1 else zero for c in range(n)], axis=2)
            for r in range(n)]
    bias_diag = jnp.concatenate(rows, axis=1)
    return bias_a, bias_diag, sub_diag


def _col_scale():
    s = np.ones((1, IN_COLS), np.float32)
    s[:, :QA_COLS] = HEAD_DIM ** -0.5
    s[:, QB_OFF:QB_OFF + QB_COLS] = HEAD_DIM ** -0.5 * LOG2E
    return jnp.asarray(s)


def kernel(x, c, ada_w, ada_b, attn_norm_g, mlp_norm_g, w_in, swa_sinks, diff_lq1, diff_lk1,
           diff_lq2, diff_lk2, diff_subln_g, w_out, rel_bias, w_up, conv_w, conv_b, w_down, final_g):
    assert x.shape == (1, SEQ, D_MODEL) and c.shape == (1, D_MODEL)
    assert DIFF_T >= 2 * BLOCK and BLOCK >= MAX_DISTANCE
    xs = x.reshape(SEQ, D_MODEL)
    mod = _ada_mod(c, ada_w, ada_b)
    bias_a, bias_diag, bias_corner = _bias_tables(rel_bias)
    col_scale = _col_scale()
    for l in range(DEPTH):
        sh_a, sc_a, g_a, sh_m, sc_m, g_m = [mod[l, :, k * D_MODEL:(k + 1) * D_MODEL] for k in range(N_MOD)]
        lam_init = 0.8 - 0.6 * math.exp(-0.3 * l)
        h = _modulate(xs, attn_norm_g[l][None], sc_a, sh_a)
        proj = _in_proj(h, w_in[l].astype(BF16), col_scale)
        ya = _swa(proj, swa_sinks[l], bias_a)
        yb = _diff_attn(proj, bias_diag, bias_corner, diff_lq1[l][None], diff_lk1[l][None],
                        diff_lq2[l][None], diff_lk2[l][None], diff_subln_g[l][None], lam_init)
        xs = _out_proj(ya, yb, w_out[l].astype(BF16), xs, g_a)
        h = _modulate(xs, mlp_norm_g[l][None], sc_m, sh_m)
        act = _mlp_up(h, w_up[l].astype(BF16), conv_w[l], conv_b[l][None])
        xs = _mlp_down(act, w_down[l].astype(BF16), xs, g_m)
    return _final_norm(xs, final_g[None]).reshape(1, SEQ, D_MODEL)
```

```python
import functools
import math

import jax
import jax.numpy as jnp
import numpy as np
from jax import lax
from jax.experimental import pallas as pl
from jax.experimental.pallas import tpu as pltpu

D_MODEL = 4096
SEQ = 16384
DEPTH = 2
HEAD_DIM = 128
SWA_Q_HEADS = 16
SWA_KV_HEADS = 2
SWA_GROUP = 8
WINDOW = 128
BLOCK = 128
DIFF_HEADS = 8
DIFF_V_DIM = 256
NUM_BUCKETS = 32
MAX_EXACT = 16
MAX_DISTANCE = 128
D_FF = 11008
CONV_WIDTH = 3
N_MOD = 6
EPS = 1e-6
NEG_INF = -1e30
LOG2E = math.log2(math.e)

QA_COLS = SWA_Q_HEADS * HEAD_DIM
KA_COLS = SWA_KV_HEADS * HEAD_DIM
VA_COLS = SWA_KV_HEADS * HEAD_DIM
QB_COLS = DIFF_HEADS * 2 * HEAD_DIM
KB_COLS = DIFF_HEADS * 2 * HEAD_DIM
VB_COLS = DIFF_HEADS * DIFF_V_DIM
KA_OFF = QA_COLS
VA_OFF = KA_OFF + KA_COLS
QB_OFF = VA_OFF + VA_COLS
KB_OFF = QB_OFF + QB_COLS
VB_OFF = KB_OFF + KB_COLS
IN_COLS = VB_OFF + VB_COLS

BF16 = jnp.bfloat16
F32 = jnp.float32

VMEM_LIMIT_BYTES = 58 * 1024 * 1024

MOD_TN = 512
NORM_TM = 512
PROJ_TM, PROJ_TN = 1024, 512
OUT_TM, OUT_TN = 1024, 512
UP_TM, UP_TN = 1024, 256
UP_CHUNK = 512
DOWN_TM, DOWN_TN = 512, 512
SWA_T = 512
DIFF_T = 512
CARRY_ROWS = 8


def _params(n_axes):
    return pltpu.CompilerParams(
        dimension_semantics=("arbitrary",) * n_axes,
        vmem_limit_bytes=VMEM_LIMIT_BYTES,
    )


def _mod_kernel(c_ref, w_ref, b_ref, o_ref):
    c = c_ref[...]
    cs = c / (1.0 + jnp.exp(-c))
    o_ref[...] = jnp.sum(cs * w_ref[...], axis=0, keepdims=True) + b_ref[...]


def _ada_mod(c, ada_w, ada_b):
    n_out = N_MOD * D_MODEL
    c_col = c.reshape(D_MODEL, 1)
    b3 = ada_b.reshape(DEPTH, 1, n_out)
    return pl.pallas_call(
        _mod_kernel,
        grid=(DEPTH, n_out // MOD_TN),
        in_specs=[
            pl.BlockSpec((D_MODEL, 1), lambda l, j: (0, 0)),
            pl.BlockSpec((None, D_MODEL, MOD_TN), lambda l, j: (l, 0, j)),
            pl.BlockSpec((None, 1, MOD_TN), lambda l, j: (l, 0, j)),
        ],
        out_specs=pl.BlockSpec((None, 1, MOD_TN), lambda l, j: (l, 0, j)),
        out_shape=jax.ShapeDtypeStruct((DEPTH, 1, n_out), F32),
        compiler_params=_params(2),
        name="ada_mod",
    )(c_col, ada_w, b3)


def _modulate_kernel(x_ref, g_ref, sc_ref, sh_ref, o_ref):
    x = x_ref[...]
    y = x * lax.rsqrt(jnp.mean(x * x, axis=-1, keepdims=True) + EPS)
    o_ref[...] = ((y * g_ref[...]) * (1.0 + sc_ref[...]) + sh_ref[...]).astype(o_ref.dtype)


def _modulate(x, g, scale, shift):
    vec = pl.BlockSpec((1, D_MODEL), lambda i: (0, 0))
    return pl.pallas_call(
        _modulate_kernel,
        grid=(SEQ // NORM_TM,),
        in_specs=[pl.BlockSpec((NORM_TM, D_MODEL), lambda i: (i, 0)), vec, vec, vec],
        out_specs=pl.BlockSpec((NORM_TM, D_MODEL), lambda i: (i, 0)),
        out_shape=jax.ShapeDtypeStruct((SEQ, D_MODEL), BF16),
        compiler_params=_params(1),
        name="modulate",
    )(x, g, scale, shift)


def _rmsnorm_kernel(x_ref, g_ref, o_ref):
    x = x_ref[...]
    y = x * lax.rsqrt(jnp.mean(x * x, axis=-1, keepdims=True) + EPS)
    o_ref[...] = y * g_ref[...]


def _final_norm(x, g):
    return pl.pallas_call(
        _rmsnorm_kernel,
        grid=(SEQ // NORM_TM,),
        in_specs=[pl.BlockSpec((NORM_TM, D_MODEL), lambda i: (i, 0)),
                  pl.BlockSpec((1, D_MODEL), lambda i: (0, 0))],
        out_specs=pl.BlockSpec((NORM_TM, D_MODEL), lambda i: (i, 0)),
        out_shape=jax.ShapeDtypeStruct((SEQ, D_MODEL), F32),
        compiler_params=_params(1),
        name="final_norm",
    )(x, g)


def _cast_weight_once(w_ref, wb_ref):
    @pl.when(pl.program_id(1) == 0)
    def _():
        wb_ref[...] = w_ref[...].astype(BF16)


def _proj_kernel(h_ref, w_ref, cs_ref, o_ref, wb_ref):
    _cast_weight_once(w_ref, wb_ref)
    acc = jnp.dot(h_ref[...], wb_ref[...], preferred_element_type=F32)
    o_ref[...] = (acc * cs_ref[...]).astype(o_ref.dtype)


def _in_proj(h, w_all, layer, col_scale):
    return pl.pallas_call(
        _proj_kernel,
        grid=(IN_COLS // PROJ_TN, SEQ // PROJ_TM),
        in_specs=[
            pl.BlockSpec((PROJ_TM, D_MODEL), lambda j, i: (i, 0)),
            pl.BlockSpec((None, D_MODEL, PROJ_TN), lambda j, i: (layer, 0, j)),
            pl.BlockSpec((1, PROJ_TN), lambda j, i: (0, j)),
        ],
        out_specs=pl.BlockSpec((PROJ_TM, PROJ_TN), lambda j, i: (i, j)),
        out_shape=jax.ShapeDtypeStruct((SEQ, IN_COLS), BF16),
        scratch_shapes=[pltpu.VMEM((D_MODEL, PROJ_TN), BF16)],
        compiler_params=_params(2),
        name="in_proj",
    )(h, w_all, col_scale)


def _swa_kernel(sink_ref, q_ref, kh_ref, km_ref, vh_ref, vm_ref, bias_ref, o_ref):
    hkv = pl.program_id(0)
    i = pl.program_id(1)
    k_all = jnp.concatenate([kh_ref[...], km_ref[...]], axis=0)
    v_all = jnp.concatenate([vh_ref[...], vm_ref[...]], axis=0)
    r = lax.broadcasted_iota(jnp.int32, (BLOCK, 2 * BLOCK), 0)
    c = lax.broadcasted_iota(jnp.int32, (BLOCK, 2 * BLOCK), 1)
    dist = BLOCK + r - c
    band_ok = (dist >= 0) & (dist < WINDOW)
    first_ok = band_ok & ((c + jnp.where(i > 0, BLOCK, 0)) >= BLOCK)
    for b in range(SWA_T // BLOCK):
        kb = k_all[b * BLOCK:(b + 2) * BLOCK]
        vb = v_all[b * BLOCK:(b + 2) * BLOCK]
        ok = first_ok if b == 0 else band_ok
        for g in range(SWA_GROUP):
            qg = q_ref[b * BLOCK:(b + 1) * BLOCK, g * HEAD_DIM:(g + 1) * HEAD_DIM]
            s = lax.dot_general(qg, kb, (((1,), (1,)), ((), ())), preferred_element_type=F32)
            s = jnp.where(ok, s + bias_ref[g], NEG_INF)
            sink = sink_ref[hkv * SWA_GROUP + g]
            m = jnp.maximum(jnp.max(s, axis=1, keepdims=True), sink)
            p = jnp.exp(s - m)
            denom = jnp.sum(p, axis=1, keepdims=True) + jnp.exp(sink - m)
            o = jnp.dot(p.astype(BF16), vb, preferred_element_type=F32) / denom
            o_ref[b * BLOCK:(b + 1) * BLOCK, g * HEAD_DIM:(g + 1) * HEAD_DIM] = o.astype(o_ref.dtype)


def _swa(proj, sinks, bias_a):
    t_blocks = SWA_T // BLOCK
    q_w = SWA_GROUP * HEAD_DIM
    k_col = KA_OFF // HEAD_DIM
    v_col = VA_OFF // HEAD_DIM

    def halo(col0):
        return lambda h, i: (jnp.maximum(i * t_blocks - 1, 0), col0 + h)

    return pl.pallas_call(
        _swa_kernel,
        grid=(SWA_KV_HEADS, SEQ // SWA_T),
        in_specs=[
            pl.BlockSpec(memory_space=pltpu.SMEM),
            pl.BlockSpec((SWA_T, q_w), lambda h, i: (i, h)),
            pl.BlockSpec((BLOCK, HEAD_DIM), halo(k_col)),
            pl.BlockSpec((SWA_T, HEAD_DIM), lambda h, i: (i, k_col + h)),
            pl.BlockSpec((BLOCK, HEAD_DIM), halo(v_col)),
            pl.BlockSpec((SWA_T, HEAD_DIM), lambda h, i: (i, v_col + h)),
            pl.BlockSpec((SWA_GROUP, BLOCK, 2 * BLOCK), lambda h, i: (h, 0, 0)),
        ],
        out_specs=pl.BlockSpec((SWA_T, q_w), lambda h, i: (i, h)),
        out_shape=jax.ShapeDtypeStruct((SEQ, QA_COLS), BF16),
        compiler_params=_params(2),
        name="swa_attn",
    )(sinks, proj, proj, proj, proj, proj, bias_a)


def _diff_kernel(q_ref, k_ref, v_ref, bdiag_ref, bcorner_ref, lq1_ref, lk1_ref, lq2_ref, lk2_ref, g_ref,
                 o_ref, m_ref, l_ref, acc_ref, s_buf, mx_buf, p_buf, a_buf, *, lam_init):
    t = DIFF_T
    n_lane = t // BLOCK
    i = pl.program_id(1)
    n_far = jnp.maximum(i - 1, 0)
    n_pairs = n_far // 2

    m_ref[...] = jnp.full(m_ref.shape, NEG_INF, F32)
    l_ref[...] = jnp.zeros(l_ref.shape, F32)
    acc_ref[...] = jnp.zeros(acc_ref.shape, F32)
    s_buf[1] = jnp.full(s_buf.shape[1:], -jnp.inf, F32)
    mx_buf[1] = jnp.full(mx_buf.shape[1:], -jnp.inf, F32)
    p_buf[0] = jnp.zeros(p_buf.shape[1:], BF16)
    a_buf[0] = jnp.ones(a_buf.shape[1:], F32)

    def lane_tiles(x):
        return [x[:, n * BLOCK:(n + 1) * BLOCK] for n in range(x.shape[1] // BLOCK)]

    def stage_a(slot, j, kind, kill=None, corner=False):
        start = pl.multiple_of(j * t, t)
        for c in range(2):
            qc = q_ref[:, c * HEAD_DIM:(c + 1) * HEAD_DIM]
            kc = k_ref[pl.ds(start, t), pl.ds(c * HEAD_DIM, HEAD_DIM)]
            s = lax.dot_general(qc, kc, (((1,), (1,)), ((), ())), preferred_element_type=F32)
            if kind == "tail":
                s = s + kill
                if corner:
                    near = s[:BLOCK, t - BLOCK:] + bcorner_ref[...]
                    top = jnp.concatenate([s[:BLOCK, :t - BLOCK], near], axis=1)
                    s = jnp.concatenate([top, s[BLOCK:]], axis=0)
            elif kind == "diag":
                row = lax.broadcasted_iota(jnp.int32, (t, t), 0)
                col = lax.broadcasted_iota(jnp.int32, (t, t), 1)
                s = jnp.where(row >= col, s + bdiag_ref[...], NEG_INF)
            s_buf[slot, c] = s
            mx_buf[slot, c] = functools.reduce(jnp.maximum, lane_tiles(s))

    def stage_b(slot):
        for c in range(2):
            m_prev = m_ref[c]
            m_new = jnp.maximum(m_prev, jnp.max(mx_buf[slot, c], axis=1, keepdims=True))
            alpha = jnp.exp2(m_prev - m_new)
            p = jnp.exp2(s_buf[slot, c] - jnp.concatenate([m_new] * n_lane, axis=1))
            l_ref[c] = alpha * l_ref[c] + functools.reduce(jnp.add, lane_tiles(p))
            p_buf[slot, c] = p.astype(BF16)
            a_buf[slot, c] = alpha
            m_ref[c] = m_new

    def stage_c(slot, j):
        start = pl.multiple_of(j * t, t)
        vj = v_ref[pl.ds(start, t), :]
        for c in range(2):
            alpha = a_buf[slot, c]
            pv = jnp.dot(p_buf[slot, c], vj, preferred_element_type=F32)
            acc_ref[c] = jnp.concatenate([alpha] * (DIFF_V_DIM // BLOCK), axis=1) * acc_ref[c] + pv

    def pair(u, carry):
        t0 = 2 * u
        stage_c(0, jnp.maximum(t0 - 2, 0))
        stage_b(1)
        stage_a(0, t0, "far")
        stage_c(1, jnp.maximum(t0 - 1, 0))
        stage_b(0)
        stage_a(1, t0 + 1, "far")
        return carry

    lax.fori_loop(0, n_pairs, pair, 0)

    e = 2 * n_pairs
    kill_e = jnp.where(n_far % 2 == 1, 0.0, -jnp.inf).astype(F32)
    prev = jnp.maximum(i - 1, 0)
    kill_p = jnp.where(i >= 1, 0.0, -jnp.inf).astype(F32)
    stage_c(0, jnp.maximum(e - 2, 0))
    stage_b(1)
    stage_a(0, e, "tail", kill_e)
    stage_c(1, jnp.maximum(e - 1, 0))
    stage_b(0)
    stage_a(1, prev, "tail", kill_p, corner=True)
    stage_c(0, e)
    stage_b(1)
    stage_a(0, i, "diag")
    stage_c(1, prev)
    stage_b(0)
    stage_c(0, i)

    lam = (jnp.exp(jnp.sum(lq1_ref[...] * lk1_ref[...], axis=1, keepdims=True))
           - jnp.exp(jnp.sum(lq2_ref[...] * lk2_ref[...], axis=1, keepdims=True)) + lam_init)
    l1 = jnp.sum(l_ref[0], axis=1, keepdims=True)
    l2 = jnp.sum(l_ref[1], axis=1, keepdims=True)
    out = acc_ref[0] / l1 - lam * (acc_ref[1] / l2)
    y = out * lax.rsqrt(jnp.mean(out * out, axis=-1, keepdims=True) + EPS)
    o_ref[...] = ((y * g_ref[...]) * (1.0 - lam_init)).astype(o_ref.dtype)


def _diff_attn(proj, bias_diag, bias_corner, lq1, lk1, lq2, lk2, subln_g, lam_init):
    t = DIFF_T
    w = 2 * HEAD_DIM
    q_col, k_col, v_col = QB_OFF // w, KB_OFF // w, VB_OFF // w
    vec = pl.BlockSpec((1, HEAD_DIM), lambda h, i: (0, 0))
    return pl.pallas_call(
        functools.partial(_diff_kernel, lam_init=lam_init),
        grid=(DIFF_HEADS, SEQ // t),
        in_specs=[
            pl.BlockSpec((t, w), lambda h, i: (i, q_col + h)),
            pl.BlockSpec((SEQ, w), lambda h, i: (0, k_col + h)),
            pl.BlockSpec((SEQ, w), lambda h, i: (0, v_col + h)),
            pl.BlockSpec((None, t, t), lambda h, i: (h, 0, 0)),
            pl.BlockSpec((None, BLOCK, BLOCK), lambda h, i: (h, 0, 0)),
            vec, vec, vec, vec,
            pl.BlockSpec((1, DIFF_V_DIM), lambda h, i: (0, 0)),
        ],
        out_specs=pl.BlockSpec((t, DIFF_V_DIM), lambda h, i: (i, h)),
        out_shape=jax.ShapeDtypeStruct((SEQ, VB_COLS), BF16),
        scratch_shapes=[
            pltpu.VMEM((2, t, BLOCK), F32),
            pltpu.VMEM((2, t, BLOCK), F32),
            pltpu.VMEM((2, t, DIFF_V_DIM), F32),
            pltpu.VMEM((2, 2, t, t), F32),
            pltpu.VMEM((2, 2, t, BLOCK), F32),
            pltpu.VMEM((2, 2, t, t), BF16),
            pltpu.VMEM((2, 2, t, BLOCK), F32),
        ],
        compiler_params=_params(2),
        name="diff_attn",
    )(proj, proj, proj, bias_diag, bias_corner, lq1, lk1, lq2, lk2, subln_g)


def _out_kernel(ya_ref, yb_ref, w_ref, x_ref, g_ref, o_ref, wb_ref):
    _cast_weight_once(w_ref, wb_ref)
    acc = jnp.dot(ya_ref[...], wb_ref[:QA_COLS, :], preferred_element_type=F32)
    acc = acc + jnp.dot(yb_ref[...], wb_ref[QA_COLS:, :], preferred_element_type=F32)
    o_ref[...] = x_ref[...] + g_ref[...] * acc


def _out_proj(ya, yb, w_all, layer, x, gate):
    return pl.pallas_call(
        _out_kernel,
        grid=(D_MODEL // OUT_TN, SEQ // OUT_TM),
        in_specs=[
            pl.BlockSpec((OUT_TM, QA_COLS), lambda j, i: (i, 0)),
            pl.BlockSpec((OUT_TM, VB_COLS), lambda j, i: (i, 0)),
            pl.BlockSpec((None, QA_COLS + VB_COLS, OUT_TN), lambda j, i: (layer, 0, j)),
            pl.BlockSpec((OUT_TM, OUT_TN), lambda j, i: (i, j)),
            pl.BlockSpec((1, OUT_TN), lambda j, i: (0, j)),
        ],
        out_specs=pl.BlockSpec((OUT_TM, OUT_TN), lambda j, i: (i, j)),
        out_shape=jax.ShapeDtypeStruct((SEQ, D_MODEL), F32),
        scratch_shapes=[pltpu.VMEM((QA_COLS + VB_COLS, OUT_TN), BF16)],
        compiler_params=_params(2),
        name="out_proj",
    )(ya, yb, w_all, x, gate)


def _up_kernel(h_ref, wg_ref, wu_ref, cw_ref, cb_ref, o_ref, carry_ref, wb_ref):
    @pl.when(pl.program_id(1) == 0)
    def _():
        carry_ref[...] = jnp.zeros(carry_ref.shape, F32)
        wb_ref[:, :UP_TN] = wg_ref[...].astype(BF16)
        wb_ref[:, UP_TN:] = wu_ref[...].astype(BF16)

    prev = carry_ref[...]
    row = lax.broadcasted_iota(jnp.int32, (CARRY_ROWS, UP_TN), 0)
    for r0 in range(0, UP_TM, UP_CHUNK):
        gu = jnp.dot(h_ref[r0:r0 + UP_CHUNK, :], wb_ref[...], preferred_element_type=F32)
        gate, up = gu[:, :UP_TN], gu[:, UP_TN:]
        conv = cb_ref[...] + gate * cw_ref[CONV_WIDTH - 1:CONV_WIDTH, :]
        for d in range(1, CONV_WIDTH):
            shifted = pltpu.roll(gate, d, 0)
            head = jnp.where(row < d, pltpu.roll(prev, d, 0), shifted[:CARRY_ROWS])
            shifted = jnp.concatenate([head, shifted[CARRY_ROWS:]], axis=0)
            conv = conv + shifted * cw_ref[CONV_WIDTH - 1 - d:CONV_WIDTH - d, :]
        act = conv / (1.0 + jnp.exp(-conv)) * up
        o_ref[r0:r0 + UP_CHUNK, :] = act.astype(o_ref.dtype)
        prev = gate[UP_CHUNK - CARRY_ROWS:]
    carry_ref[...] = prev


def _mlp_up(h, w_all, layer, conv_w, conv_b):
    n_blocks = D_FF // UP_TN
    return pl.pallas_call(
        _up_kernel,
        grid=(n_blocks, SEQ // UP_TM),
        in_specs=[
            pl.BlockSpec((UP_TM, D_MODEL), lambda j, i: (i, 0)),
            pl.BlockSpec((None, D_MODEL, UP_TN), lambda j, i: (layer, 0, j)),
            pl.BlockSpec((None, D_MODEL, UP_TN), lambda j, i: (layer, 0, n_blocks + j)),
            pl.BlockSpec((CONV_WIDTH, UP_TN), lambda j, i: (0, j)),
            pl.BlockSpec((1, UP_TN), lambda j, i: (0, j)),
        ],
        out_specs=pl.BlockSpec((UP_TM, UP_TN), lambda j, i: (i, j)),
        out_shape=jax.ShapeDtypeStruct((SEQ, D_FF), BF16),
        scratch_shapes=[pltpu.VMEM((CARRY_ROWS, UP_TN), F32), pltpu.VMEM((D_MODEL, 2 * UP_TN), BF16)],
        compiler_params=_params(2),
        name="mlp_up",
    )(h, w_all, w_all, conv_w, conv_b)


def _down_kernel(a_ref, w_ref, x_ref, g_ref, o_ref):
    acc = jnp.dot(a_ref[...], w_ref[...], preferred_element_type=F32)
    o_ref[...] = x_ref[...] + g_ref[...] * acc


def _mlp_down(act, w, x, gate):
    return pl.pallas_call(
        _down_kernel,
        grid=(D_MODEL // DOWN_TN, SEQ // DOWN_TM),
        in_specs=[
            pl.BlockSpec((DOWN_TM, D_FF), lambda j, i: (i, 0)),
            pl.BlockSpec((D_FF, DOWN_TN), lambda j, i: (0, j)),
            pl.BlockSpec((DOWN_TM, DOWN_TN), lambda j, i: (i, j)),
            pl.BlockSpec((1, DOWN_TN), lambda j, i: (0, j)),
        ],
        out_specs=pl.BlockSpec((DOWN_TM, DOWN_TN), lambda j, i: (i, j)),
        out_shape=jax.ShapeDtypeStruct((SEQ, D_MODEL), F32),
        compiler_params=_params(2),
        name="mlp_down",
    )(act, w, x, gate)


def _t5_bucket(n):
    n = jnp.maximum(n, 0)
    nf = jnp.maximum(n, 1).astype(F32)
    large = MAX_EXACT + (jnp.log(nf / MAX_EXACT) / math.log(MAX_DISTANCE / MAX_EXACT)
                         * (NUM_BUCKETS - MAX_EXACT)).astype(jnp.int32)
    large = jnp.minimum(large, NUM_BUCKETS - 1)
    return jnp.where(n < MAX_EXACT, n, large)


def _toeplitz_tiles(rel):
    n_heads = rel.shape[1]
    period = 3 * BLOCK
    u = np.arange(period)
    by_dist = rel[_t5_bucket(jnp.asarray(u - (BLOCK - 1)))]
    w = by_dist[(2 * BLOCK - 1 - u) % period].T
    skew = jnp.tile(w, (1, BLOCK))[:, :BLOCK * (period - 1)].reshape(n_heads, BLOCK, period - 1)
    return skew[:, :, :2 * BLOCK]


def _bias_tables(rel_bias):
    tiles = _toeplitz_tiles(rel_bias.astype(F32))
    bias_a = tiles[:SWA_Q_HEADS]
    far = rel_bias[NUM_BUCKETS - 1, SWA_Q_HEADS:].astype(F32)
    tb = (tiles[SWA_Q_HEADS:] - far[:, None, None]) * LOG2E
    sub_diag, on_diag = tb[:, :, :BLOCK], tb[:, :, BLOCK:]
    n = DIFF_T // BLOCK
    zero = jnp.zeros_like(on_diag)
    rows = [jnp.concatenate([on_diag if r == c else sub_diag if r == c + 1 else zero for c in range(n)], axis=2)
            for r in range(n)]
    bias_diag = jnp.concatenate(rows, axis=1)
    return bias_a, bias_diag, sub_diag


def _col_scale():
    s = np.ones((1, IN_COLS), np.float32)
    s[:, :QA_COLS] = HEAD_DIM ** -0.5
    s[:, QB_OFF:QB_OFF + QB_COLS] = HEAD_DIM ** -0.5 * LOG2E
    return jnp.asarray(s)


def kernel(x, c, ada_w, ada_b, attn_norm_g, mlp_norm_g, w_in, swa_sinks, diff_lq1, diff_lk1,
           diff_lq2, diff_lk2, diff_subln_g, w_out, rel_bias, w_up, conv_w, conv_b, w_down, final_g):
    assert x.shape == (1, SEQ, D_MODEL) and c.shape == (1, D_MODEL)
    assert DIFF_T >= 2 * BLOCK and BLOCK >= MAX_DISTANCE
    xs = x.reshape(SEQ, D_MODEL)
    mod = _ada_mod(c, ada_w, ada_b)
    bias_a, bias_diag, bias_corner = _bias_tables(rel_bias)
    col_scale = _col_scale()
    for l in range(DEPTH):
        sh_a, sc_a, g_a, sh_m, sc_m, g_m = [mod[l, :, k * D_MODEL:(k + 1) * D_MODEL] for k in range(N_MOD)]
        lam_init = 0.8 - 0.6 * math.exp(-0.3 * l)
        h = _modulate(xs, attn_norm_g[l][None], sc_a, sh_a)
        proj = _in_proj(h, w_in, l, col_scale)
        ya = _swa(proj, swa_sinks[l], bias_a)
        yb = _diff_attn(proj, bias_diag, bias_corner, diff_lq1[l][None], diff_lk1[l][None],
                        diff_lq2[l][None], diff_lk2[l][None], diff_subln_g[l][None], lam_init)
        xs = _out_proj(ya, yb, w_out, l, xs, g_a)
        h = _modulate(xs, mlp_norm_g[l][None], sc_m, sh_m)
        act = _mlp_up(h, w_up, l, conv_w[l], conv_b[l][None])
        xs = _mlp_down(act, w_down[l].astype(BF16), xs, g_m)
    return _final_norm(xs, final_g[None]).reshape(1, SEQ, D_MODEL)
```

```python
import functools
import math

import jax
import jax.numpy as jnp
import numpy as np
from jax import lax
from jax.experimental import pallas as pl
from jax.experimental.pallas import tpu as pltpu

D_MODEL = 4096
SEQ = 16384
DEPTH = 2
HEAD_DIM = 128
SWA_Q_HEADS = 16
SWA_KV_HEADS = 2
SWA_GROUP = 8
WINDOW = 128
BLOCK = 128
DIFF_HEADS = 8
DIFF_V_DIM = 256
NUM_BUCKETS = 32
MAX_EXACT = 16
MAX_DISTANCE = 128
D_FF = 11008
CONV_WIDTH = 3
N_MOD = 6
EPS = 1e-6
NEG_INF = -1e30
LOG2E = math.log2(math.e)

QA_COLS = SWA_Q_HEADS * HEAD_DIM
KA_COLS = SWA_KV_HEADS * HEAD_DIM
VA_COLS = SWA_KV_HEADS * HEAD_DIM
QB_COLS = DIFF_HEADS * 2 * HEAD_DIM
KB_COLS = DIFF_HEADS * 2 * HEAD_DIM
VB_COLS = DIFF_HEADS * DIFF_V_DIM
KA_OFF = QA_COLS
VA_OFF = KA_OFF + KA_COLS
QB_OFF = VA_OFF + VA_COLS
KB_OFF = QB_OFF + QB_COLS
VB_OFF = KB_OFF + KB_COLS
IN_COLS = VB_OFF + VB_COLS

BF16 = jnp.bfloat16
F32 = jnp.float32

VMEM_LIMIT_BYTES = 58 * 1024 * 1024

MOD_TN = 512
NORM_TM = 512
PROJ_TM, PROJ_TN = 1024, 512
OUT_TM, OUT_TN = 1024, 512
UP_TM, UP_TN = 1024, 256
MM_CHUNK = 128
DOWN_TM, DOWN_TN = 512, 512
SWA_T = 512
DIFF_T = 512
CARRY_ROWS = 8


def _row_chunks(rows):
    return [pl.ds(r0, MM_CHUNK) for r0 in range(0, rows, MM_CHUNK)]


def _params(n_axes):
    return pltpu.CompilerParams(
        dimension_semantics=("arbitrary",) * n_axes,
        vmem_limit_bytes=VMEM_LIMIT_BYTES,
    )


def _mod_kernel(c_ref, w_ref, b_ref, o_ref):
    c = c_ref[...]
    cs = c / (1.0 + jnp.exp(-c))
    o_ref[...] = jnp.sum(cs * w_ref[...], axis=0, keepdims=True) + b_ref[...]


def _ada_mod(c, ada_w, ada_b):
    n_out = N_MOD * D_MODEL
    c_col = c.reshape(D_MODEL, 1)
    b3 = ada_b.reshape(DEPTH, 1, n_out)
    return pl.pallas_call(
        _mod_kernel,
        grid=(DEPTH, n_out // MOD_TN),
        in_specs=[
            pl.BlockSpec((D_MODEL, 1), lambda l, j: (0, 0)),
            pl.BlockSpec((None, D_MODEL, MOD_TN), lambda l, j: (l, 0, j)),
            pl.BlockSpec((None, 1, MOD_TN), lambda l, j: (l, 0, j)),
        ],
        out_specs=pl.BlockSpec((None, 1, MOD_TN), lambda l, j: (l, 0, j)),
        out_shape=jax.ShapeDtypeStruct((DEPTH, 1, n_out), F32),
        compiler_params=_params(2),
        name="ada_mod",
    )(c_col, ada_w, b3)


def _modulate_kernel(x_ref, g_ref, sc_ref, sh_ref, o_ref):
    x = x_ref[...]
    y = x * lax.rsqrt(jnp.mean(x * x, axis=-1, keepdims=True) + EPS)
    o_ref[...] = ((y * g_ref[...]) * (1.0 + sc_ref[...]) + sh_ref[...]).astype(o_ref.dtype)


def _modulate(x, g, scale, shift):
    vec = pl.BlockSpec((1, D_MODEL), lambda i: (0, 0))
    return pl.pallas_call(
        _modulate_kernel,
        grid=(SEQ // NORM_TM,),
        in_specs=[pl.BlockSpec((NORM_TM, D_MODEL), lambda i: (i, 0)), vec, vec, vec],
        out_specs=pl.BlockSpec((NORM_TM, D_MODEL), lambda i: (i, 0)),
        out_shape=jax.ShapeDtypeStruct((SEQ, D_MODEL), BF16),
        compiler_params=_params(1),
        name="modulate",
    )(x, g, scale, shift)


def _rmsnorm_kernel(x_ref, g_ref, o_ref):
    x = x_ref[...]
    y = x * lax.rsqrt(jnp.mean(x * x, axis=-1, keepdims=True) + EPS)
    o_ref[...] = y * g_ref[...]


def _final_norm(x, g):
    return pl.pallas_call(
        _rmsnorm_kernel,
        grid=(SEQ // NORM_TM,),
        in_specs=[pl.BlockSpec((NORM_TM, D_MODEL), lambda i: (i, 0)),
                  pl.BlockSpec((1, D_MODEL), lambda i: (0, 0))],
        out_specs=pl.BlockSpec((NORM_TM, D_MODEL), lambda i: (i, 0)),
        out_shape=jax.ShapeDtypeStruct((SEQ, D_MODEL), F32),
        compiler_params=_params(1),
        name="final_norm",
    )(x, g)


def _cast_weight_once(w_ref, wb_ref):
    @pl.when(pl.program_id(1) == 0)
    def _():
        wb_ref[...] = w_ref[...].astype(BF16)


def _proj_kernel(h_ref, w_ref, cs_ref, o_ref, wb_ref):
    _cast_weight_once(w_ref, wb_ref)
    acc = jnp.dot(h_ref[...], wb_ref[...], preferred_element_type=F32)
    o_ref[...] = (acc * cs_ref[...]).astype(o_ref.dtype)


def _in_proj(h, w_all, layer, col_scale):
    return pl.pallas_call(
        _proj_kernel,
        grid=(IN_COLS // PROJ_TN, SEQ // PROJ_TM),
        in_specs=[
            pl.BlockSpec((PROJ_TM, D_MODEL), lambda j, i: (i, 0)),
            pl.BlockSpec((None, D_MODEL, PROJ_TN), lambda j, i: (layer, 0, j)),
            pl.BlockSpec((1, PROJ_TN), lambda j, i: (0, j)),
        ],
        out_specs=pl.BlockSpec((PROJ_TM, PROJ_TN), lambda j, i: (i, j)),
        out_shape=jax.ShapeDtypeStruct((SEQ, IN_COLS), BF16),
        scratch_shapes=[pltpu.VMEM((D_MODEL, PROJ_TN), BF16)],
        compiler_params=_params(2),
        name="in_proj",
    )(h, w_all, col_scale)


def _swa_kernel(sink_ref, q_ref, kh_ref, km_ref, vh_ref, vm_ref, bias_ref, o_ref):
    hkv = pl.program_id(0)
    i = pl.program_id(1)
    k_all = jnp.concatenate([kh_ref[...], km_ref[...]], axis=0)
    v_all = jnp.concatenate([vh_ref[...], vm_ref[...]], axis=0)
    r = lax.broadcasted_iota(jnp.int32, (BLOCK, 2 * BLOCK), 0)
    c = lax.broadcasted_iota(jnp.int32, (BLOCK, 2 * BLOCK), 1)
    dist = BLOCK + r - c
    band_ok = (dist >= 0) & (dist < WINDOW)
    first_ok = band_ok & ((c + jnp.where(i > 0, BLOCK, 0)) >= BLOCK)
    for b in range(SWA_T // BLOCK):
        kb = k_all[b * BLOCK:(b + 2) * BLOCK]
        vb = v_all[b * BLOCK:(b + 2) * BLOCK]
        ok = first_ok if b == 0 else band_ok
        for g in range(SWA_GROUP):
            qg = q_ref[b * BLOCK:(b + 1) * BLOCK, g * HEAD_DIM:(g + 1) * HEAD_DIM]
            s = lax.dot_general(qg, kb, (((1,), (1,)), ((), ())), preferred_element_type=F32)
            s = jnp.where(ok, s + bias_ref[g], NEG_INF)
            sink = sink_ref[hkv * SWA_GROUP + g]
            m = jnp.maximum(jnp.max(s, axis=1, keepdims=True), sink)
            p = jnp.exp(s - m)
            denom = jnp.sum(p, axis=1, keepdims=True) + jnp.exp(sink - m)
            o = jnp.dot(p.astype(BF16), vb, preferred_element_type=F32) / denom
            o_ref[b * BLOCK:(b + 1) * BLOCK, g * HEAD_DIM:(g + 1) * HEAD_DIM] = o.astype(o_ref.dtype)


def _swa(proj, sinks, bias_a):
    t_blocks = SWA_T // BLOCK
    q_w = SWA_GROUP * HEAD_DIM
    k_col = KA_OFF // HEAD_DIM
    v_col = VA_OFF // HEAD_DIM

    def halo(col0):
        return lambda h, i: (jnp.maximum(i * t_blocks - 1, 0), col0 + h)

    return pl.pallas_call(
        _swa_kernel,
        grid=(SWA_KV_HEADS, SEQ // SWA_T),
        in_specs=[
            pl.BlockSpec(memory_space=pltpu.SMEM),
            pl.BlockSpec((SWA_T, q_w), lambda h, i: (i, h)),
            pl.BlockSpec((BLOCK, HEAD_DIM), halo(k_col)),
            pl.BlockSpec((SWA_T, HEAD_DIM), lambda h, i: (i, k_col + h)),
            pl.BlockSpec((BLOCK, HEAD_DIM), halo(v_col)),
            pl.BlockSpec((SWA_T, HEAD_DIM), lambda h, i: (i, v_col + h)),
            pl.BlockSpec((SWA_GROUP, BLOCK, 2 * BLOCK), lambda h, i: (h, 0, 0)),
        ],
        out_specs=pl.BlockSpec((SWA_T, q_w), lambda h, i: (i, h)),
        out_shape=jax.ShapeDtypeStruct((SEQ, QA_COLS), BF16),
        compiler_params=_params(2),
        name="swa_attn",
    )(sinks, proj, proj, proj, proj, proj, bias_a)


def _diff_kernel(q_ref, k_ref, v_ref, bdiag_ref, bcorner_ref, lq1_ref, lk1_ref, lq2_ref, lk2_ref, g_ref,
                 o_ref, m_ref, l_ref, acc_ref, s_buf, mx_buf, p_buf, a_buf, *, lam_init):
    t = DIFF_T
    n_lane = t // BLOCK
    i = pl.program_id(1)
    n_far = jnp.maximum(i - 1, 0)
    n_pairs = n_far // 2

    m_ref[...] = jnp.full(m_ref.shape, NEG_INF, F32)
    l_ref[...] = jnp.zeros(l_ref.shape, F32)
    acc_ref[...] = jnp.zeros(acc_ref.shape, F32)
    s_buf[1] = jnp.full(s_buf.shape[1:], -jnp.inf, F32)
    mx_buf[1] = jnp.full(mx_buf.shape[1:], -jnp.inf, F32)
    p_buf[0] = jnp.zeros(p_buf.shape[1:], BF16)
    a_buf[0] = jnp.ones(a_buf.shape[1:], F32)

    def lane_tiles(x):
        return [x[:, n * BLOCK:(n + 1) * BLOCK] for n in range(x.shape[1] // BLOCK)]

    def stage_a(slot, j, kind, kill=None):
        start = pl.multiple_of(j * t, t)
        for c in range(2):
            qc = q_ref[:, c * HEAD_DIM:(c + 1) * HEAD_DIM]
            kc = k_ref[pl.ds(start, t), pl.ds(c * HEAD_DIM, HEAD_DIM)]
            s = lax.dot_general(qc, kc, (((1,), (1,)), ((), ())), preferred_element_type=F32)
            if kind == "prev":
                if kill is not None:
                    s = s + kill
                near = s[:BLOCK, t - BLOCK:] + bcorner_ref[...]
                top = jnp.concatenate([s[:BLOCK, :t - BLOCK], near], axis=1)
                s = jnp.concatenate([top, s[BLOCK:]], axis=0)
            elif kind == "diag":
                row = lax.broadcasted_iota(jnp.int32, (t, t), 0)
                col = lax.broadcasted_iota(jnp.int32, (t, t), 1)
                s = jnp.where(row >= col, s + bdiag_ref[...], NEG_INF)
            s_buf[slot, c] = s
            mx_buf[slot, c] = functools.reduce(jnp.maximum, lane_tiles(s))

    def stage_b(slot):
        for c in range(2):
            m_prev = m_ref[c]
            m_new = jnp.maximum(m_prev, jnp.max(mx_buf[slot, c], axis=1, keepdims=True))
            alpha = jnp.exp2(m_prev - m_new)
            p = jnp.exp2(s_buf[slot, c] - jnp.concatenate([m_new] * n_lane, axis=1))
            l_ref[c] = alpha * l_ref[c] + functools.reduce(jnp.add, lane_tiles(p))
            p_buf[slot, c] = p.astype(BF16)
            a_buf[slot, c] = alpha
            m_ref[c] = m_new

    def stage_c(slot, j):
        start = pl.multiple_of(j * t, t)
        vj = v_ref[pl.ds(start, t), :]
        for c in range(2):
            alpha = a_buf[slot, c]
            pv = jnp.dot(p_buf[slot, c], vj, preferred_element_type=F32)
            acc_ref[c] = jnp.concatenate([alpha] * (DIFF_V_DIM // BLOCK), axis=1) * acc_ref[c] + pv

    def pair(u, carry):
        t0 = 2 * u
        stage_c(0, jnp.maximum(t0 - 2, 0))
        stage_b(1)
        stage_a(0, t0, "far")
        stage_c(1, jnp.maximum(t0 - 1, 0))
        stage_b(0)
        stage_a(1, t0 + 1, "far")
        return carry

    lax.fori_loop(0, n_pairs, pair, 0)

    e = 2 * n_pairs
    prev = jnp.maximum(i - 1, 0)
    odd = n_far % 2 == 1

    @pl.when(odd)
    def _():
        stage_c(0, jnp.maximum(e - 2, 0))
        stage_b(1)
        stage_a(0, e, "far")
        stage_c(1, jnp.maximum(e - 1, 0))
        stage_b(0)
        stage_a(1, prev, "prev")
        stage_c(0, e)
        stage_b(1)
        stage_a(0, i, "diag")
        stage_c(1, prev)
        stage_b(0)
        stage_c(0, i)

    @pl.when(jnp.logical_not(odd))
    def _():
        kill = jnp.where(i >= 1, 0.0, -jnp.inf).astype(F32)
        stage_c(0, jnp.maximum(e - 2, 0))
        stage_b(1)
        stage_a(0, prev, "prev", kill)
        stage_c(1, jnp.maximum(e - 1, 0))
        stage_b(0)
        stage_a(1, i, "diag")
        stage_c(0, prev)
        stage_b(1)
        stage_c(1, i)

    lam = (jnp.exp(jnp.sum(lq1_ref[...] * lk1_ref[...], axis=1, keepdims=True))
           - jnp.exp(jnp.sum(lq2_ref[...] * lk2_ref[...], axis=1, keepdims=True)) + lam_init)
    l1 = jnp.sum(l_ref[0], axis=1, keepdims=True)
    l2 = jnp.sum(l_ref[1], axis=1, keepdims=True)
    out = acc_ref[0] / l1 - lam * (acc_ref[1] / l2)
    y = out * lax.rsqrt(jnp.mean(out * out, axis=-1, keepdims=True) + EPS)
    o_ref[...] = ((y * g_ref[...]) * (1.0 - lam_init)).astype(o_ref.dtype)


def _diff_attn(proj, bias_diag, bias_corner, lq1, lk1, lq2, lk2, subln_g, lam_init):
    t = DIFF_T
    w = 2 * HEAD_DIM
    q_col, k_col, v_col = QB_OFF // w, KB_OFF // w, VB_OFF // w
    vec = pl.BlockSpec((1, HEAD_DIM), lambda h, i: (0, 0))
    return pl.pallas_call(
        functools.partial(_diff_kernel, lam_init=lam_init),
        grid=(DIFF_HEADS, SEQ // t),
        in_specs=[
            pl.BlockSpec((t, w), lambda h, i: (i, q_col + h)),
            pl.BlockSpec((SEQ, w), lambda h, i: (0, k_col + h)),
            pl.BlockSpec((SEQ, w), lambda h, i: (0, v_col + h)),
            pl.BlockSpec((None, t, t), lambda h, i: (h, 0, 0)),
            pl.BlockSpec((None, BLOCK, BLOCK), lambda h, i: (h, 0, 0)),
            vec, vec, vec, vec,
            pl.BlockSpec((1, DIFF_V_DIM), lambda h, i: (0, 0)),
        ],
        out_specs=pl.BlockSpec((t, DIFF_V_DIM), lambda h, i: (i, h)),
        out_shape=jax.ShapeDtypeStruct((SEQ, VB_COLS), BF16),
        scratch_shapes=[
            pltpu.VMEM((2, t, BLOCK), F32),
            pltpu.VMEM((2, t, BLOCK), F32),
            pltpu.VMEM((2, t, DIFF_V_DIM), F32),
            pltpu.VMEM((2, 2, t, t), F32),
            pltpu.VMEM((2, 2, t, BLOCK), F32),
            pltpu.VMEM((2, 2, t, t), BF16),
            pltpu.VMEM((2, 2, t, BLOCK), F32),
        ],
        compiler_params=_params(2),
        name="diff_attn",
    )(proj, proj, proj, bias_diag, bias_corner, lq1, lk1, lq2, lk2, subln_g)


def _out_kernel(ya_ref, yb_ref, w_ref, x_ref, g_ref, o_ref, wb_ref):
    _cast_weight_once(w_ref, wb_ref)
    acc = jnp.dot(ya_ref[...], wb_ref[:QA_COLS, :], preferred_element_type=F32)
    acc = acc + jnp.dot(yb_ref[...], wb_ref[QA_COLS:, :], preferred_element_type=F32)
    o_ref[...] = x_ref[...] + g_ref[...] * acc


def _out_proj(ya, yb, w_all, layer, x, gate):
    return pl.pallas_call(
        _out_kernel,
        grid=(D_MODEL // OUT_TN, SEQ // OUT_TM),
        in_specs=[
            pl.BlockSpec((OUT_TM, QA_COLS), lambda j, i: (i, 0)),
            pl.BlockSpec((OUT_TM, VB_COLS), lambda j, i: (i, 0)),
            pl.BlockSpec((None, QA_COLS + VB_COLS, OUT_TN), lambda j, i: (layer, 0, j)),
            pl.BlockSpec((OUT_TM, OUT_TN), lambda j, i: (i, j)),
            pl.BlockSpec((1, OUT_TN), lambda j, i: (0, j)),
        ],
        out_specs=pl.BlockSpec((OUT_TM, OUT_TN), lambda j, i: (i, j)),
        out_shape=jax.ShapeDtypeStruct((SEQ, D_MODEL), F32),
        scratch_shapes=[pltpu.VMEM((QA_COLS + VB_COLS, OUT_TN), BF16)],
        compiler_params=_params(2),
        name="out_proj",
    )(ya, yb, w_all, x, gate)


def _up_kernel(h_ref, wg_ref, wu_ref, cw_ref, cb_ref, o_ref, carry_ref, wb_ref, raw_ref, *, n_rows):
    s = pl.program_id(0)
    last = pl.num_programs(0) - 2
    row_mm = jnp.minimum(s, last) % n_rows
    row_ep = jnp.maximum(s - 1, 0) % n_rows

    @pl.when(s == 0)
    def _():
        raw_ref[...] = jnp.zeros(raw_ref.shape, F32)

    @pl.when(row_mm == 0)
    def _():
        wb_ref[:, :UP_TN] = wg_ref[...].astype(BF16)
        wb_ref[:, UP_TN:] = wu_ref[...].astype(BF16)

    @pl.when(row_ep == 0)
    def _():
        carry_ref[...] = jnp.zeros(carry_ref.shape, F32)

    prev = carry_ref[...]
    row = lax.broadcasted_iota(jnp.int32, (CARRY_ROWS, UP_TN), 0)
    gate = raw_ref[:, :UP_TN]
    up = raw_ref[:, UP_TN:]
    conv = cb_ref[...] + gate * cw_ref[CONV_WIDTH - 1:CONV_WIDTH, :]
    for d in range(1, CONV_WIDTH):
        shifted = pltpu.roll(gate, d, 0)
        head = jnp.where(row < d, pltpu.roll(prev, d, 0), shifted[:CARRY_ROWS])
        shifted = jnp.concatenate([head, shifted[CARRY_ROWS:]], axis=0)
        conv = conv + shifted * cw_ref[CONV_WIDTH - 1 - d:CONV_WIDTH - d, :]
    act = conv / (1.0 + jnp.exp(-conv)) * up
    o_ref[...] = act.astype(o_ref.dtype)
    carry_ref[...] = gate[UP_TM - CARRY_ROWS:]

    for r in _row_chunks(UP_TM):
        raw_ref[r, :] = jnp.dot(h_ref[r, :], wb_ref[...], preferred_element_type=F32)


def _mlp_up(h, w_all, layer, conv_w, conv_b):
    n_cols = D_FF // UP_TN
    n_rows = SEQ // UP_TM
    last = n_cols * n_rows - 1

    def mm_row(s):
        return jnp.minimum(s, last) % n_rows

    def mm_col(s):
        return jnp.minimum(s, last) // n_rows

    def ep_row(s):
        return jnp.maximum(s - 1, 0) % n_rows

    def ep_col(s):
        return jnp.maximum(s - 1, 0) // n_rows

    return pl.pallas_call(
        functools.partial(_up_kernel, n_rows=n_rows),
        grid=(last + 2,),
        in_specs=[
            pl.BlockSpec((UP_TM, D_MODEL), lambda s: (mm_row(s), 0)),
            pl.BlockSpec((None, D_MODEL, UP_TN), lambda s: (layer, 0, mm_col(s))),
            pl.BlockSpec((None, D_MODEL, UP_TN), lambda s: (layer, 0, n_cols + mm_col(s))),
            pl.BlockSpec((CONV_WIDTH, UP_TN), lambda s: (0, ep_col(s))),
            pl.BlockSpec((1, UP_TN), lambda s: (0, ep_col(s))),
        ],
        out_specs=pl.BlockSpec((UP_TM, UP_TN), lambda s: (ep_row(s), ep_col(s))),
        out_shape=jax.ShapeDtypeStruct((SEQ, D_FF), BF16),
        scratch_shapes=[pltpu.VMEM((CARRY_ROWS, UP_TN), F32), pltpu.VMEM((D_MODEL, 2 * UP_TN), BF16),
                        pltpu.VMEM((UP_TM, 2 * UP_TN), F32)],
        compiler_params=_params(1),
        name="mlp_up",
    )(h, w_all, w_all, conv_w, conv_b)


def _down_kernel(a_ref, w_ref, x_ref, g_ref, o_ref):
    acc = jnp.dot(a_ref[...], w_ref[...], preferred_element_type=F32)
    o_ref[...] = x_ref[...] + g_ref[...] * acc


def _mlp_down(act, w, x, gate):
    return pl.pallas_call(
        _down_kernel,
        grid=(D_MODEL // DOWN_TN, SEQ // DOWN_TM),
        in_specs=[
            pl.BlockSpec((DOWN_TM, D_FF), lambda j, i: (i, 0)),
            pl.BlockSpec((D_FF, DOWN_TN), lambda j, i: (0, j)),
            pl.BlockSpec((DOWN_TM, DOWN_TN), lambda j, i: (i, j)),
            pl.BlockSpec((1, DOWN_TN), lambda j, i: (0, j)),
        ],
        out_specs=pl.BlockSpec((DOWN_TM, DOWN_TN), lambda j, i: (i, j)),
        out_shape=jax.ShapeDtypeStruct((SEQ, D_MODEL), F32),
        compiler_params=_params(2),
        name="mlp_down",
    )(act, w, x, gate)


def _t5_bucket(n):
    n = jnp.maximum(n, 0)
    nf = jnp.maximum(n, 1).astype(F32)
    large = MAX_EXACT + (jnp.log(nf / MAX_EXACT) / math.log(MAX_DISTANCE / MAX_EXACT)
                         * (NUM_BUCKETS - MAX_EXACT)).astype(jnp.int32)
    large = jnp.minimum(large, NUM_BUCKETS - 1)
    return jnp.where(n < MAX_EXACT, n, large)


def _toeplitz_tiles(rel):
    n_heads = rel.shape[1]
    period = 3 * BLOCK
    u = np.arange(period)
    by_dist = rel[_t5_bucket(jnp.asarray(u - (BLOCK - 1)))]
    w = by_dist[(2 * BLOCK - 1 - u) % period].T
    skew = jnp.tile(w, (1, BLOCK))[:, :BLOCK * (period - 1)].reshape(n_heads, BLOCK, period - 1)
    return skew[:, :, :2 * BLOCK]


def _bias_tables(rel_bias):
    tiles = _toeplitz_tiles(rel_bias.astype(F32))
    bias_a = tiles[:SWA_Q_HEADS]
    far = rel_bias[NUM_BUCKETS - 1, SWA_Q_HEADS:].astype(F32)
    tb = (tiles[SWA_Q_HEADS:] - far[:, None, None]) * LOG2E
    sub_diag, on_diag = tb[:, :, :BLOCK], tb[:, :, BLOCK:]
    n = DIFF_T // BLOCK
    zero = jnp.zeros_like(on_diag)
    rows = [jnp.concatenate([on_diag if r == c else sub_diag if r == c + 1 else zero for c in range(n)], axis=2)
            for r in range(n)]
    bias_diag = jnp.concatenate(rows, axis=1)
    return bias_a, bias_diag, sub_diag


def _col_scale():
    s = np.ones((1, IN_COLS), np.float32)
    s[:, :QA_COLS] = HEAD_DIM ** -0.5
    s[:, QB_OFF:QB_OFF + QB_COLS] = HEAD_DIM ** -0.5 * LOG2E
    return jnp.asarray(s)


def kernel(x, c, ada_w, ada_b, attn_norm_g, mlp_norm_g, w_in, swa_sinks, diff_lq1, diff_lk1,
           diff_lq2, diff_lk2, diff_subln_g, w_out, rel_bias, w_up, conv_w, conv_b, w_down, final_g):
    assert x.shape == (1, SEQ, D_MODEL) and c.shape == (1, D_MODEL)
    assert DIFF_T >= 2 * BLOCK and BLOCK >= MAX_DISTANCE
    xs = x.reshape(SEQ, D_MODEL)
    mod = _ada_mod(c, ada_w, ada_b)
    bias_a, bias_diag, bias_corner = _bias_tables(rel_bias)
    col_scale = _col_scale()
    for l in range(DEPTH):
        sh_a, sc_a, g_a, sh_m, sc_m, g_m = [mod[l, :, k * D_MODEL:(k + 1) * D_MODEL] for k in range(N_MOD)]
        lam_init = 0.8 - 0.6 * math.exp(-0.3 * l)
        h = _modulate(xs, attn_norm_g[l][None], sc_a, sh_a)
        proj = _in_proj(h, w_in, l, col_scale)
        ya = _swa(proj, swa_sinks[l], bias_a)
        yb = _diff_attn(proj, bias_diag, bias_corner, diff_lq1[l][None], diff_lk1[l][None],
                        diff_lq2[l][None], diff_lk2[l][None], diff_subln_g[l][None], lam_init)
        xs = _out_proj(ya, yb, w_out, l, xs, g_a)
        h = _modulate(xs, mlp_norm_g[l][None], sc_m, sh_m)
        act = _mlp_up(h, w_up, l, conv_w[l], conv_b[l][None])
        xs = _mlp_down(act, w_down[l].astype(BF16), xs, g_m)
    return _final_norm(xs, final_g[None]).reshape(1, SEQ, D_MODEL)
```

```python
import functools
import math

import jax
import jax.numpy as jnp
import numpy as np
from jax import lax
from jax.experimental import pallas as pl
from jax.experimental.pallas import tpu as pltpu

D_MODEL = 4096
SEQ = 16384
DEPTH = 2
HEAD_DIM = 128
SWA_Q_HEADS = 16
SWA_KV_HEADS = 2
SWA_GROUP = 8
WINDOW = 128
BLOCK = 128
DIFF_HEADS = 8
DIFF_V_DIM = 256
NUM_BUCKETS = 32
MAX_EXACT = 16
MAX_DISTANCE = 128
D_FF = 11008
CONV_WIDTH = 3
N_MOD = 6
EPS = 1e-6
NEG_INF = -1e30
LOG2E = math.log2(math.e)

QA_COLS = SWA_Q_HEADS * HEAD_DIM
KA_COLS = SWA_KV_HEADS * HEAD_DIM
VA_COLS = SWA_KV_HEADS * HEAD_DIM
QB_COLS = DIFF_HEADS * 2 * HEAD_DIM
KB_COLS = DIFF_HEADS * 2 * HEAD_DIM
VB_COLS = DIFF_HEADS * DIFF_V_DIM
KA_OFF = QA_COLS
VA_OFF = KA_OFF + KA_COLS
QB_OFF = VA_OFF + VA_COLS
KB_OFF = QB_OFF + QB_COLS
VB_OFF = KB_OFF + KB_COLS
IN_COLS = VB_OFF + VB_COLS

BF16 = jnp.bfloat16
F32 = jnp.float32

VMEM_LIMIT_BYTES = 58 * 1024 * 1024

MOD_TN = 512
NORM_TM = 512
PROJ_TM, PROJ_TN = 1024, 512
OUT_TM, OUT_TN = 1024, 512
UP_TM, UP_TN = 1024, 256
MM_CHUNK = 128
DOWN_TM, DOWN_TN = 512, 512
CAST_STEPS = 16
SWA_T = 512
DIFF_T = 512
CARRY_ROWS = 8


def _row_chunks(rows):
    return [pl.ds(r0, MM_CHUNK) for r0 in range(0, rows, MM_CHUNK)]


def _params(n_axes):
    return pltpu.CompilerParams(
        dimension_semantics=("arbitrary",) * n_axes,
        vmem_limit_bytes=VMEM_LIMIT_BYTES,
    )


def _mod_kernel(c_ref, w_ref, b_ref, o_ref):
    c = c_ref[...]
    cs = c / (1.0 + jnp.exp(-c))
    o_ref[...] = jnp.sum(cs * w_ref[...], axis=0, keepdims=True) + b_ref[...]


def _ada_mod(c, ada_w, ada_b):
    n_out = N_MOD * D_MODEL
    c_col = c.reshape(D_MODEL, 1)
    b3 = ada_b.reshape(DEPTH, 1, n_out)
    return pl.pallas_call(
        _mod_kernel,
        grid=(DEPTH, n_out // MOD_TN),
        in_specs=[
            pl.BlockSpec((D_MODEL, 1), lambda l, j: (0, 0)),
            pl.BlockSpec((None, D_MODEL, MOD_TN), lambda l, j: (l, 0, j)),
            pl.BlockSpec((None, 1, MOD_TN), lambda l, j: (l, 0, j)),
        ],
        out_specs=pl.BlockSpec((None, 1, MOD_TN), lambda l, j: (l, 0, j)),
        out_shape=jax.ShapeDtypeStruct((DEPTH, 1, n_out), F32),
        compiler_params=_params(2),
        name="ada_mod",
    )(c_col, ada_w, b3)


def _modulate_kernel(x_ref, g_ref, sc_ref, sh_ref, o_ref):
    x = x_ref[...]
    y = x * lax.rsqrt(jnp.mean(x * x, axis=-1, keepdims=True) + EPS)
    o_ref[...] = ((y * g_ref[...]) * (1.0 + sc_ref[...]) + sh_ref[...]).astype(o_ref.dtype)


def _modulate(x, g, scale, shift):
    vec = pl.BlockSpec((1, D_MODEL), lambda i: (0, 0))
    return pl.pallas_call(
        _modulate_kernel,
        grid=(SEQ // NORM_TM,),
        in_specs=[pl.BlockSpec((NORM_TM, D_MODEL), lambda i: (i, 0)), vec, vec, vec],
        out_specs=pl.BlockSpec((NORM_TM, D_MODEL), lambda i: (i, 0)),
        out_shape=jax.ShapeDtypeStruct((SEQ, D_MODEL), BF16),
        compiler_params=_params(1),
        name="modulate",
    )(x, g, scale, shift)


def _rmsnorm_kernel(x_ref, g_ref, o_ref):
    x = x_ref[...]
    y = x * lax.rsqrt(jnp.mean(x * x, axis=-1, keepdims=True) + EPS)
    o_ref[...] = y * g_ref[...]


def _final_norm(x, g):
    return pl.pallas_call(
        _rmsnorm_kernel,
        grid=(SEQ // NORM_TM,),
        in_specs=[pl.BlockSpec((NORM_TM, D_MODEL), lambda i: (i, 0)),
                  pl.BlockSpec((1, D_MODEL), lambda i: (0, 0))],
        out_specs=pl.BlockSpec((NORM_TM, D_MODEL), lambda i: (i, 0)),
        out_shape=jax.ShapeDtypeStruct((SEQ, D_MODEL), F32),
        compiler_params=_params(1),
        name="final_norm",
    )(x, g)


def _cast_weight_once(w_ref, wb_ref):
    @pl.when(pl.program_id(1) == 0)
    def _():
        wb_ref[...] = w_ref[...].astype(BF16)


def _proj_kernel(h_ref, w_ref, cs_ref, o_ref, wb_ref):
    _cast_weight_once(w_ref, wb_ref)
    acc = jnp.dot(h_ref[...], wb_ref[...], preferred_element_type=F32)
    o_ref[...] = (acc * cs_ref[...]).astype(o_ref.dtype)


def _in_proj(h, w_all, layer, col_scale):
    return pl.pallas_call(
        _proj_kernel,
        grid=(IN_COLS // PROJ_TN, SEQ // PROJ_TM),
        in_specs=[
            pl.BlockSpec((PROJ_TM, D_MODEL), lambda j, i: (i, 0)),
            pl.BlockSpec((None, D_MODEL, PROJ_TN), lambda j, i: (layer, 0, j)),
            pl.BlockSpec((1, PROJ_TN), lambda j, i: (0, j)),
        ],
        out_specs=pl.BlockSpec((PROJ_TM, PROJ_TN), lambda j, i: (i, j)),
        out_shape=jax.ShapeDtypeStruct((SEQ, IN_COLS), BF16),
        scratch_shapes=[pltpu.VMEM((D_MODEL, PROJ_TN), BF16)],
        compiler_params=_params(2),
        name="in_proj",
    )(h, w_all, col_scale)


def _swa_kernel(sink_ref, q_ref, kh_ref, km_ref, vh_ref, vm_ref, bias_ref, o_ref):
    hkv = pl.program_id(0)
    i = pl.program_id(1)
    k_all = jnp.concatenate([kh_ref[...], km_ref[...]], axis=0)
    v_all = jnp.concatenate([vh_ref[...], vm_ref[...]], axis=0)
    r = lax.broadcasted_iota(jnp.int32, (BLOCK, 2 * BLOCK), 0)
    c = lax.broadcasted_iota(jnp.int32, (BLOCK, 2 * BLOCK), 1)
    dist = BLOCK + r - c
    band_ok = (dist >= 0) & (dist < WINDOW)
    first_ok = band_ok & ((c + jnp.where(i > 0, BLOCK, 0)) >= BLOCK)
    for b in range(SWA_T // BLOCK):
        kb = k_all[b * BLOCK:(b + 2) * BLOCK]
        vb = v_all[b * BLOCK:(b + 2) * BLOCK]
        ok = first_ok if b == 0 else band_ok
        for g in range(SWA_GROUP):
            qg = q_ref[b * BLOCK:(b + 1) * BLOCK, g * HEAD_DIM:(g + 1) * HEAD_DIM]
            s = lax.dot_general(qg, kb, (((1,), (1,)), ((), ())), preferred_element_type=F32)
            s = jnp.where(ok, s + bias_ref[g], NEG_INF)
            sink = sink_ref[hkv * SWA_GROUP + g]
            m = jnp.maximum(jnp.max(s, axis=1, keepdims=True), sink)
            p = jnp.exp(s - m)
            denom = jnp.sum(p, axis=1, keepdims=True) + jnp.exp(sink - m)
            o = jnp.dot(p.astype(BF16), vb, preferred_element_type=F32) / denom
            o_ref[b * BLOCK:(b + 1) * BLOCK, g * HEAD_DIM:(g + 1) * HEAD_DIM] = o.astype(o_ref.dtype)


def _swa(proj, sinks, bias_a):
    t_blocks = SWA_T // BLOCK
    q_w = SWA_GROUP * HEAD_DIM
    k_col = KA_OFF // HEAD_DIM
    v_col = VA_OFF // HEAD_DIM

    def halo(col0):
        return lambda h, i: (jnp.maximum(i * t_blocks - 1, 0), col0 + h)

    return pl.pallas_call(
        _swa_kernel,
        grid=(SWA_KV_HEADS, SEQ // SWA_T),
        in_specs=[
            pl.BlockSpec(memory_space=pltpu.SMEM),
            pl.BlockSpec((SWA_T, q_w), lambda h, i: (i, h)),
            pl.BlockSpec((BLOCK, HEAD_DIM), halo(k_col)),
            pl.BlockSpec((SWA_T, HEAD_DIM), lambda h, i: (i, k_col + h)),
            pl.BlockSpec((BLOCK, HEAD_DIM), halo(v_col)),
            pl.BlockSpec((SWA_T, HEAD_DIM), lambda h, i: (i, v_col + h)),
            pl.BlockSpec((SWA_GROUP, BLOCK, 2 * BLOCK), lambda h, i: (h, 0, 0)),
        ],
        out_specs=pl.BlockSpec((SWA_T, q_w), lambda h, i: (i, h)),
        out_shape=jax.ShapeDtypeStruct((SEQ, QA_COLS), BF16),
        compiler_params=_params(2),
        name="swa_attn",
    )(sinks, proj, proj, proj, proj, proj, bias_a)


def _diff_kernel(q_ref, k_ref, v_ref, bdiag_ref, bcorner_ref, lq1_ref, lk1_ref, lq2_ref, lk2_ref, g_ref,
                 o_ref, m_ref, l_ref, acc_ref, s_buf, mx_buf, p_buf, a_buf, *, lam_init):
    t = DIFF_T
    n_lane = t // BLOCK
    i = pl.program_id(1)
    n_far = jnp.maximum(i - 1, 0)
    n_pairs = n_far // 2

    m_ref[...] = jnp.full(m_ref.shape, NEG_INF, F32)
    l_ref[...] = jnp.zeros(l_ref.shape, F32)
    acc_ref[...] = jnp.zeros(acc_ref.shape, F32)

    def lane_tiles(x):
        return [x[:, n * BLOCK:(n + 1) * BLOCK] for n in range(x.shape[1] // BLOCK)]

    def stage_a(slot, j, kind, kill=None):
        start = pl.multiple_of(j * t, t)
        for c in range(2):
            qc = q_ref[:, c * HEAD_DIM:(c + 1) * HEAD_DIM]
            kc = k_ref[pl.ds(start, t), pl.ds(c * HEAD_DIM, HEAD_DIM)]
            s = lax.dot_general(qc, kc, (((1,), (1,)), ((), ())), preferred_element_type=F32)
            if kind == "prev":
                if kill is not None:
                    s = s + kill
                near = s[:BLOCK, t - BLOCK:] + bcorner_ref[...]
                top = jnp.concatenate([s[:BLOCK, :t - BLOCK], near], axis=1)
                s = jnp.concatenate([top, s[BLOCK:]], axis=0)
            elif kind == "diag":
                row = lax.broadcasted_iota(jnp.int32, (t, t), 0)
                col = lax.broadcasted_iota(jnp.int32, (t, t), 1)
                s = jnp.where(row >= col, s + bdiag_ref[...], NEG_INF)
            s_buf[slot, c] = s
            mx_buf[slot, c] = functools.reduce(jnp.maximum, lane_tiles(s))

    def stage_b(slot):
        for c in range(2):
            m_prev = m_ref[c]
            m_new = jnp.maximum(m_prev, jnp.max(mx_buf[slot, c], axis=1, keepdims=True))
            alpha = jnp.exp2(m_prev - m_new)
            p = jnp.exp2(s_buf[slot, c] - jnp.concatenate([m_new] * n_lane, axis=1))
            l_ref[c] = alpha * l_ref[c] + functools.reduce(jnp.add, lane_tiles(p))
            p_buf[slot, c] = p.astype(BF16)
            a_buf[slot, c] = alpha
            m_ref[c] = m_new

    def stage_c(slot, j):
        start = pl.multiple_of(j * t, t)
        vj = v_ref[pl.ds(start, t), :]
        for c in range(2):
            alpha = a_buf[slot, c]
            pv = jnp.dot(p_buf[slot, c], vj, preferred_element_type=F32)
            acc_ref[c] = jnp.concatenate([alpha] * (DIFF_V_DIM // BLOCK), axis=1) * acc_ref[c] + pv

    def pair(u, carry):
        t0 = 2 * u
        stage_c(0, jnp.maximum(t0 - 2, 0))
        stage_b(1)
        stage_a(0, t0, "far")
        stage_c(1, jnp.maximum(t0 - 1, 0))
        stage_b(0)
        stage_a(1, t0 + 1, "far")
        return carry

    @pl.when(n_pairs >= 1)
    def _():
        stage_a(0, 0, "far")
        stage_b(0)
        stage_a(1, 1, "far")

    @pl.when(n_pairs == 0)
    def _():
        s_buf[1] = jnp.full(s_buf.shape[1:], -jnp.inf, F32)
        mx_buf[1] = jnp.full(mx_buf.shape[1:], -jnp.inf, F32)
        p_buf[0] = jnp.zeros(p_buf.shape[1:], BF16)
        a_buf[0] = jnp.ones(a_buf.shape[1:], F32)

    lax.fori_loop(jnp.minimum(n_pairs, 1), n_pairs, pair, 0)

    e = 2 * n_pairs
    prev = jnp.maximum(i - 1, 0)
    odd = n_far % 2 == 1

    @pl.when(odd)
    def _():
        stage_c(0, jnp.maximum(e - 2, 0))
        stage_b(1)
        stage_a(0, e, "far")
        stage_c(1, jnp.maximum(e - 1, 0))
        stage_b(0)
        stage_a(1, prev, "prev")
        stage_c(0, e)
        stage_b(1)
        stage_a(0, i, "diag")
        stage_c(1, prev)
        stage_b(0)
        stage_c(0, i)

    @pl.when(jnp.logical_not(odd))
    def _():
        kill = jnp.where(i >= 1, 0.0, -jnp.inf).astype(F32)
        stage_c(0, jnp.maximum(e - 2, 0))
        stage_b(1)
        stage_a(0, prev, "prev", kill)
        stage_c(1, jnp.maximum(e - 1, 0))
        stage_b(0)
        stage_a(1, i, "diag")
        stage_c(0, prev)
        stage_b(1)
        stage_c(1, i)

    lam = (jnp.exp(jnp.sum(lq1_ref[...] * lk1_ref[...], axis=1, keepdims=True))
           - jnp.exp(jnp.sum(lq2_ref[...] * lk2_ref[...], axis=1, keepdims=True)) + lam_init)
    l1 = jnp.sum(l_ref[0], axis=1, keepdims=True)
    l2 = jnp.sum(l_ref[1], axis=1, keepdims=True)
    out = acc_ref[0] / l1 - lam * (acc_ref[1] / l2)
    y = out * lax.rsqrt(jnp.mean(out * out, axis=-1, keepdims=True) + EPS)
    o_ref[...] = ((y * g_ref[...]) * (1.0 - lam_init)).astype(o_ref.dtype)


def _diff_attn(proj, bias_diag, bias_corner, lq1, lk1, lq2, lk2, subln_g, lam_init):
    t = DIFF_T
    w = 2 * HEAD_DIM
    q_col, k_col, v_col = QB_OFF // w, KB_OFF // w, VB_OFF // w
    vec = pl.BlockSpec((1, HEAD_DIM), lambda h, i: (0, 0))
    return pl.pallas_call(
        functools.partial(_diff_kernel, lam_init=lam_init),
        grid=(DIFF_HEADS, SEQ // t),
        in_specs=[
            pl.BlockSpec((t, w), lambda h, i: (i, q_col + h)),
            pl.BlockSpec((SEQ, w), lambda h, i: (0, k_col + h)),
            pl.BlockSpec((SEQ, w), lambda h, i: (0, v_col + h)),
            pl.BlockSpec((None, t, t), lambda h, i: (h, 0, 0)),
            pl.BlockSpec((None, BLOCK, BLOCK), lambda h, i: (h, 0, 0)),
            vec, vec, vec, vec,
            pl.BlockSpec((1, DIFF_V_DIM), lambda h, i: (0, 0)),
        ],
        out_specs=pl.BlockSpec((t, DIFF_V_DIM), lambda h, i: (i, h)),
        out_shape=jax.ShapeDtypeStruct((SEQ, VB_COLS), BF16),
        scratch_shapes=[
            pltpu.VMEM((2, t, BLOCK), F32),
            pltpu.VMEM((2, t, BLOCK), F32),
            pltpu.VMEM((2, t, DIFF_V_DIM), F32),
            pltpu.VMEM((2, 2, t, t), F32),
            pltpu.VMEM((2, 2, t, BLOCK), F32),
            pltpu.VMEM((2, 2, t, t), BF16),
            pltpu.VMEM((2, 2, t, BLOCK), F32),
        ],
        compiler_params=_params(2),
        name="diff_attn",
    )(proj, proj, proj, bias_diag, bias_corner, lq1, lk1, lq2, lk2, subln_g)


def _out_kernel(ya_ref, yb_ref, w_ref, x_ref, g_ref, o_ref, wb_ref):
    _cast_weight_once(w_ref, wb_ref)
    acc = jnp.dot(ya_ref[...], wb_ref[:QA_COLS, :], preferred_element_type=F32)
    acc = acc + jnp.dot(yb_ref[...], wb_ref[QA_COLS:, :], preferred_element_type=F32)
    o_ref[...] = x_ref[...] + g_ref[...] * acc


def _out_proj(ya, yb, w_all, layer, x, gate):
    return pl.pallas_call(
        _out_kernel,
        grid=(D_MODEL // OUT_TN, SEQ // OUT_TM),
        in_specs=[
            pl.BlockSpec((OUT_TM, QA_COLS), lambda j, i: (i, 0)),
            pl.BlockSpec((OUT_TM, VB_COLS), lambda j, i: (i, 0)),
            pl.BlockSpec((None, QA_COLS + VB_COLS, OUT_TN), lambda j, i: (layer, 0, j)),
            pl.BlockSpec((OUT_TM, OUT_TN), lambda j, i: (i, j)),
            pl.BlockSpec((1, OUT_TN), lambda j, i: (0, j)),
        ],
        out_specs=pl.BlockSpec((OUT_TM, OUT_TN), lambda j, i: (i, j)),
        out_shape=jax.ShapeDtypeStruct((SEQ, D_MODEL), F32),
        scratch_shapes=[pltpu.VMEM((QA_COLS + VB_COLS, OUT_TN), BF16)],
        compiler_params=_params(2),
        name="out_proj",
    )(ya, yb, w_all, x, gate)


def _up_kernel(h_ref, wg_ref, wu_ref, cw_ref, cb_ref, o_ref, carry_ref, wb_ref, raw_ref, *, n_rows):
    s = pl.program_id(0)
    last = pl.num_programs(0) - 2
    row_mm = jnp.minimum(s, last) % n_rows
    row_ep = jnp.maximum(s - 1, 0) % n_rows

    @pl.when(s == 0)
    def _():
        raw_ref[...] = jnp.zeros(raw_ref.shape, F32)

    @pl.when(row_mm == 0)
    def _():
        wb_ref[:, :UP_TN] = wg_ref[...].astype(BF16)
        wb_ref[:, UP_TN:] = wu_ref[...].astype(BF16)

    @pl.when(row_ep == 0)
    def _():
        carry_ref[...] = jnp.zeros(carry_ref.shape, F32)

    prev = carry_ref[...]
    row = lax.broadcasted_iota(jnp.int32, (CARRY_ROWS, UP_TN), 0)
    gate = raw_ref[:, :UP_TN]
    up = raw_ref[:, UP_TN:]
    conv = cb_ref[...] + gate * cw_ref[CONV_WIDTH - 1:CONV_WIDTH, :]
    for d in range(1, CONV_WIDTH):
        shifted = pltpu.roll(gate, d, 0)
        head = jnp.where(row < d, pltpu.roll(prev, d, 0), shifted[:CARRY_ROWS])
        shifted = jnp.concatenate([head, shifted[CARRY_ROWS:]], axis=0)
        conv = conv + shifted * cw_ref[CONV_WIDTH - 1 - d:CONV_WIDTH - d, :]
    act = conv / (1.0 + jnp.exp(-conv)) * up
    o_ref[...] = act.astype(o_ref.dtype)
    carry_ref[...] = gate[UP_TM - CARRY_ROWS:]

    for r in _row_chunks(UP_TM):
        raw_ref[r, :] = jnp.dot(h_ref[r, :], wb_ref[...], preferred_element_type=F32)


def _mlp_up(h, w_all, layer, conv_w, conv_b):
    n_cols = D_FF // UP_TN
    n_rows = SEQ // UP_TM
    last = n_cols * n_rows - 1

    def mm_row(s):
        return jnp.minimum(s, last) % n_rows

    def mm_col(s):
        return jnp.minimum(s, last) // n_rows

    def ep_row(s):
        return jnp.maximum(s - 1, 0) % n_rows

    def ep_col(s):
        return jnp.maximum(s - 1, 0) // n_rows

    return pl.pallas_call(
        functools.partial(_up_kernel, n_rows=n_rows),
        grid=(last + 2,),
        in_specs=[
            pl.BlockSpec((UP_TM, D_MODEL), lambda s: (mm_row(s), 0)),
            pl.BlockSpec((None, D_MODEL, UP_TN), lambda s: (layer, 0, mm_col(s))),
            pl.BlockSpec((None, D_MODEL, UP_TN), lambda s: (layer, 0, n_cols + mm_col(s))),
            pl.BlockSpec((CONV_WIDTH, UP_TN), lambda s: (0, ep_col(s))),
            pl.BlockSpec((1, UP_TN), lambda s: (0, ep_col(s))),
        ],
        out_specs=pl.BlockSpec((UP_TM, UP_TN), lambda s: (ep_row(s), ep_col(s))),
        out_shape=jax.ShapeDtypeStruct((SEQ, D_FF), BF16),
        scratch_shapes=[pltpu.VMEM((CARRY_ROWS, UP_TN), F32), pltpu.VMEM((D_MODEL, 2 * UP_TN), BF16),
                        pltpu.VMEM((UP_TM, 2 * UP_TN), F32)],
        compiler_params=_params(1),
        name="mlp_up",
    )(h, w_all, w_all, conv_w, conv_b)


def _cast_kernel(w_ref, o_ref):
    o_ref[...] = w_ref[...].astype(o_ref.dtype)


def _down_weight_bf16(w_all, layer):
    rows = D_FF // CAST_STEPS
    return pl.pallas_call(
        _cast_kernel,
        grid=(CAST_STEPS,),
        in_specs=[pl.BlockSpec((None, rows, D_MODEL), lambda i: (layer, i, 0))],
        out_specs=pl.BlockSpec((rows, D_MODEL), lambda i: (i, 0)),
        out_shape=jax.ShapeDtypeStruct((D_FF, D_MODEL), BF16),
        compiler_params=_params(1),
        name="cast_w_down",
    )(w_all)


def _down_kernel(a_ref, w_ref, x_ref, g_ref, o_ref):
    acc = jnp.dot(a_ref[...], w_ref[...], preferred_element_type=F32)
    o_ref[...] = x_ref[...] + g_ref[...] * acc


def _mlp_down(act, w, x, gate):
    return pl.pallas_call(
        _down_kernel,
        grid=(D_MODEL // DOWN_TN, SEQ // DOWN_TM),
        in_specs=[
            pl.BlockSpec((DOWN_TM, D_FF), lambda j, i: (i, 0)),
            pl.BlockSpec((D_FF, DOWN_TN), lambda j, i: (0, j)),
            pl.BlockSpec((DOWN_TM, DOWN_TN), lambda j, i: (i, j)),
            pl.BlockSpec((1, DOWN_TN), lambda j, i: (0, j)),
        ],
        out_specs=pl.BlockSpec((DOWN_TM, DOWN_TN), lambda j, i: (i, j)),
        out_shape=jax.ShapeDtypeStruct((SEQ, D_MODEL), F32),
        compiler_params=_params(2),
        name="mlp_down",
    )(act, w, x, gate)


def _t5_bucket(n):
    n = jnp.maximum(n, 0)
    nf = jnp.maximum(n, 1).astype(F32)
    large = MAX_EXACT + (jnp.log(nf / MAX_EXACT) / math.log(MAX_DISTANCE / MAX_EXACT)
                         * (NUM_BUCKETS - MAX_EXACT)).astype(jnp.int32)
    large = jnp.minimum(large, NUM_BUCKETS - 1)
    return jnp.where(n < MAX_EXACT, n, large)


def _toeplitz_tiles(rel):
    n_heads = rel.shape[1]
    period = 3 * BLOCK
    u = np.arange(period)
    by_dist = rel[_t5_bucket(jnp.asarray(u - (BLOCK - 1)))]
    w = by_dist[(2 * BLOCK - 1 - u) % period].T
    skew = jnp.tile(w, (1, BLOCK))[:, :BLOCK * (period - 1)].reshape(n_heads, BLOCK, period - 1)
    return skew[:, :, :2 * BLOCK]


def _bias_tables(rel_bias):
    tiles = _toeplitz_tiles(rel_bias.astype(F32))
    bias_a = tiles[:SWA_Q_HEADS]
    far = rel_bias[NUM_BUCKETS - 1, SWA_Q_HEADS:].astype(F32)
    tb = (tiles[SWA_Q_HEADS:] - far[:, None, None]) * LOG2E
    sub_diag, on_diag = tb[:, :, :BLOCK], tb[:, :, BLOCK:]
    n = DIFF_T // BLOCK
    zero = jnp.zeros_like(on_diag)
    rows = [jnp.concatenate([on_diag if r == c else sub_diag if r == c + 1 else zero for c in range(n)], axis=2)
            for r in range(n)]
    bias_diag = jnp.concatenate(rows, axis=1)
    return bias_a, bias_diag, sub_diag


def _col_scale():
    s = np.ones((1, IN_COLS), np.float32)
    s[:, :QA_COLS] = HEAD_DIM ** -0.5
    s[:, QB_OFF:QB_OFF + QB_COLS] = HEAD_DIM ** -0.5 * LOG2E
    return jnp.asarray(s)


def kernel(x, c, ada_w, ada_b, attn_norm_g, mlp_norm_g, w_in, swa_sinks, diff_lq1, diff_lk1,
           diff_lq2, diff_lk2, diff_subln_g, w_out, rel_bias, w_up, conv_w, conv_b, w_down, final_g):
    assert x.shape == (1, SEQ, D_MODEL) and c.shape == (1, D_MODEL)
    assert DIFF_T >= 2 * BLOCK and BLOCK >= MAX_DISTANCE
    xs = x.reshape(SEQ, D_MODEL)
    mod = _ada_mod(c, ada_w, ada_b)
    bias_a, bias_diag, bias_corner = _bias_tables(rel_bias)
    col_scale = _col_scale()
    for l in range(DEPTH):
        sh_a, sc_a, g_a, sh_m, sc_m, g_m = [mod[l, :, k * D_MODEL:(k + 1) * D_MODEL] for k in range(N_MOD)]
        lam_init = 0.8 - 0.6 * math.exp(-0.3 * l)
        h = _modulate(xs, attn_norm_g[l][None], sc_a, sh_a)
        proj = _in_proj(h, w_in, l, col_scale)
        ya = _swa(proj, swa_sinks[l], bias_a)
        yb = _diff_attn(proj, bias_diag, bias_corner, diff_lq1[l][None], diff_lk1[l][None],
                        diff_lq2[l][None], diff_lk2[l][None], diff_subln_g[l][None], lam_init)
        xs = _out_proj(ya, yb, w_out, l, xs, g_a)
        h = _modulate(xs, mlp_norm_g[l][None], sc_m, sh_m)
        act = _mlp_up(h, w_up, l, conv_w[l], conv_b[l][None])
        xs = _mlp_down(act, _down_weight_bf16(w_down, l), xs, g_m)
    return _final_norm(xs, final_g[None]).reshape(1, SEQ, D_MODEL)
```

```python
import functools
import math

import jax
import jax.numpy as jnp
import numpy as np
from jax import lax
from jax.experimental import pallas as pl
from jax.experimental.pallas import tpu as pltpu

D_MODEL = 4096
SEQ = 16384
DEPTH = 2
HEAD_DIM = 128
SWA_Q_HEADS = 16
SWA_KV_HEADS = 2
SWA_GROUP = 8
WINDOW = 128
BLOCK = 128
DIFF_HEADS = 8
DIFF_V_DIM = 256
NUM_BUCKETS = 32
MAX_EXACT = 16
MAX_DISTANCE = 128
D_FF = 11008
CONV_WIDTH = 3
N_MOD = 6
EPS = 1e-6
NEG_INF = -1e30
LOG2E = math.log2(math.e)

QA_COLS = SWA_Q_HEADS * HEAD_DIM
KA_COLS = SWA_KV_HEADS * HEAD_DIM
VA_COLS = SWA_KV_HEADS * HEAD_DIM
QB_COLS = DIFF_HEADS * 2 * HEAD_DIM
KB_COLS = DIFF_HEADS * 2 * HEAD_DIM
VB_COLS = DIFF_HEADS * DIFF_V_DIM
KA_OFF = QA_COLS
VA_OFF = KA_OFF + KA_COLS
QB_OFF = VA_OFF + VA_COLS
KB_OFF = QB_OFF + QB_COLS
VB_OFF = KB_OFF + KB_COLS
IN_COLS = VB_OFF + VB_COLS

BF16 = jnp.bfloat16
F32 = jnp.float32

VMEM_LIMIT_BYTES = 58 * 1024 * 1024

MOD_TN = 512
NORM_TM = 512
PROJ_TM, PROJ_TN = 1024, 512
OUT_TM, OUT_TN = 1024, 512
UP_TM, UP_TN = 1024, 256
MM_CHUNK = 128
DOWN_TM, DOWN_TN = 512, 512
CAST_STEPS = 16
SWA_T = 512
DIFF_T = 512
CARRY_ROWS = 8


def _row_chunks(rows):
    return [pl.ds(r0, MM_CHUNK) for r0 in range(0, rows, MM_CHUNK)]


def _params(n_axes):
    return pltpu.CompilerParams(
        dimension_semantics=("arbitrary",) * n_axes,
        vmem_limit_bytes=VMEM_LIMIT_BYTES,
    )


def _mod_kernel(c_ref, w_ref, b_ref, o_ref):
    c = c_ref[...]
    cs = c / (1.0 + jnp.exp(-c))
    o_ref[...] = jnp.sum(cs * w_ref[...], axis=0, keepdims=True) + b_ref[...]


def _ada_mod(c, ada_w, ada_b):
    n_out = N_MOD * D_MODEL
    c_col = c.reshape(D_MODEL, 1)
    b3 = ada_b.reshape(DEPTH, 1, n_out)
    return pl.pallas_call(
        _mod_kernel,
        grid=(DEPTH, n_out // MOD_TN),
        in_specs=[
            pl.BlockSpec((D_MODEL, 1), lambda l, j: (0, 0)),
            pl.BlockSpec((None, D_MODEL, MOD_TN), lambda l, j: (l, 0, j)),
            pl.BlockSpec((None, 1, MOD_TN), lambda l, j: (l, 0, j)),
        ],
        out_specs=pl.BlockSpec((None, 1, MOD_TN), lambda l, j: (l, 0, j)),
        out_shape=jax.ShapeDtypeStruct((DEPTH, 1, n_out), F32),
        compiler_params=_params(2),
        name="ada_mod",
    )(c_col, ada_w, b3)


def _modulate_kernel(x_ref, g_ref, sc_ref, sh_ref, o_ref):
    x = x_ref[...]
    y = x * lax.rsqrt(jnp.mean(x * x, axis=-1, keepdims=True) + EPS)
    o_ref[...] = ((y * g_ref[...]) * (1.0 + sc_ref[...]) + sh_ref[...]).astype(o_ref.dtype)


def _modulate(x, g, scale, shift):
    vec = pl.BlockSpec((1, D_MODEL), lambda i: (0, 0))
    return pl.pallas_call(
        _modulate_kernel,
        grid=(SEQ // NORM_TM,),
        in_specs=[pl.BlockSpec((NORM_TM, D_MODEL), lambda i: (i, 0)), vec, vec, vec],
        out_specs=pl.BlockSpec((NORM_TM, D_MODEL), lambda i: (i, 0)),
        out_shape=jax.ShapeDtypeStruct((SEQ, D_MODEL), BF16),
        compiler_params=_params(1),
        name="modulate",
    )(x, g, scale, shift)


def _rmsnorm_kernel(x_ref, g_ref, o_ref):
    x = x_ref[...]
    y = x * lax.rsqrt(jnp.mean(x * x, axis=-1, keepdims=True) + EPS)
    o_ref[...] = y * g_ref[...]


def _final_norm(x, g):
    return pl.pallas_call(
        _rmsnorm_kernel,
        grid=(SEQ // NORM_TM,),
        in_specs=[pl.BlockSpec((NORM_TM, D_MODEL), lambda i: (i, 0)),
                  pl.BlockSpec((1, D_MODEL), lambda i: (0, 0))],
        out_specs=pl.BlockSpec((NORM_TM, D_MODEL), lambda i: (i, 0)),
        out_shape=jax.ShapeDtypeStruct((SEQ, D_MODEL), F32),
        compiler_params=_params(1),
        name="final_norm",
    )(x, g)


def _cast_weight_once(w_ref, wb_ref):
    @pl.when(pl.program_id(1) == 0)
    def _():
        wb_ref[...] = w_ref[...].astype(BF16)


def _proj_kernel(h_ref, w_ref, cs_ref, o_ref, wb_ref):
    _cast_weight_once(w_ref, wb_ref)
    acc = jnp.dot(h_ref[...], wb_ref[...], preferred_element_type=F32)
    o_ref[...] = (acc * cs_ref[...]).astype(o_ref.dtype)


def _in_proj(h, w_all, layer, col_scale):
    return pl.pallas_call(
        _proj_kernel,
        grid=(IN_COLS // PROJ_TN, SEQ // PROJ_TM),
        in_specs=[
            pl.BlockSpec((PROJ_TM, D_MODEL), lambda j, i: (i, 0)),
            pl.BlockSpec((None, D_MODEL, PROJ_TN), lambda j, i: (layer, 0, j)),
            pl.BlockSpec((1, PROJ_TN), lambda j, i: (0, j)),
        ],
        out_specs=pl.BlockSpec((PROJ_TM, PROJ_TN), lambda j, i: (i, j)),
        out_shape=jax.ShapeDtypeStruct((SEQ, IN_COLS), BF16),
        scratch_shapes=[pltpu.VMEM((D_MODEL, PROJ_TN), BF16)],
        compiler_params=_params(2),
        name="in_proj",
    )(h, w_all, col_scale)


def _swa_kernel(sink_ref, q_ref, kh_ref, km_ref, vh_ref, vm_ref, bias_ref, o_ref):
    hkv = pl.program_id(0)
    i = pl.program_id(1)
    k_all = jnp.concatenate([kh_ref[...], km_ref[...]], axis=0)
    v_all = jnp.concatenate([vh_ref[...], vm_ref[...]], axis=0)
    r = lax.broadcasted_iota(jnp.int32, (BLOCK, 2 * BLOCK), 0)
    c = lax.broadcasted_iota(jnp.int32, (BLOCK, 2 * BLOCK), 1)
    dist = BLOCK + r - c
    band_ok = (dist >= 0) & (dist < WINDOW)
    first_ok = band_ok & ((c + jnp.where(i > 0, BLOCK, 0)) >= BLOCK)
    for b in range(SWA_T // BLOCK):
        kb = k_all[b * BLOCK:(b + 2) * BLOCK]
        vb = v_all[b * BLOCK:(b + 2) * BLOCK]
        ok = first_ok if b == 0 else band_ok
        for g in range(SWA_GROUP):
            qg = q_ref[b * BLOCK:(b + 1) * BLOCK, g * HEAD_DIM:(g + 1) * HEAD_DIM]
            s = lax.dot_general(qg, kb, (((1,), (1,)), ((), ())), preferred_element_type=F32)
            s = jnp.where(ok, s + bias_ref[g], NEG_INF)
            sink = sink_ref[hkv * SWA_GROUP + g]
            m = jnp.maximum(jnp.max(s, axis=1, keepdims=True), sink)
            p = jnp.exp(s - m)
            denom = jnp.sum(p, axis=1, keepdims=True) + jnp.exp(sink - m)
            o = jnp.dot(p.astype(BF16), vb, preferred_element_type=F32) / denom
            o_ref[b * BLOCK:(b + 1) * BLOCK, g * HEAD_DIM:(g + 1) * HEAD_DIM] = o.astype(o_ref.dtype)


def _swa(proj, sinks, bias_a):
    t_blocks = SWA_T // BLOCK
    q_w = SWA_GROUP * HEAD_DIM
    k_col = KA_OFF // HEAD_DIM
    v_col = VA_OFF // HEAD_DIM

    def halo(col0):
        return lambda h, i: (jnp.maximum(i * t_blocks - 1, 0), col0 + h)

    return pl.pallas_call(
        _swa_kernel,
        grid=(SWA_KV_HEADS, SEQ // SWA_T),
        in_specs=[
            pl.BlockSpec(memory_space=pltpu.SMEM),
            pl.BlockSpec((SWA_T, q_w), lambda h, i: (i, h)),
            pl.BlockSpec((BLOCK, HEAD_DIM), halo(k_col)),
            pl.BlockSpec((SWA_T, HEAD_DIM), lambda h, i: (i, k_col + h)),
            pl.BlockSpec((BLOCK, HEAD_DIM), halo(v_col)),
            pl.BlockSpec((SWA_T, HEAD_DIM), lambda h, i: (i, v_col + h)),
            pl.BlockSpec((SWA_GROUP, BLOCK, 2 * BLOCK), lambda h, i: (h, 0, 0)),
        ],
        out_specs=pl.BlockSpec((SWA_T, q_w), lambda h, i: (i, h)),
        out_shape=jax.ShapeDtypeStruct((SEQ, QA_COLS), BF16),
        compiler_params=_params(2),
        name="swa_attn",
    )(sinks, proj, proj, proj, proj, proj, bias_a)


def _diff_kernel(q_ref, k_ref, v_ref, bdiag_ref, bcorner_ref, lq1_ref, lk1_ref, lq2_ref, lk2_ref, g_ref,
                 o_ref, m_ref, l_ref, acc_ref, s_buf, mx_buf, p_buf, a_buf, *, lam_init):
    t = DIFF_T
    n_lane = t // BLOCK
    i = pl.program_id(1)
    n_far = jnp.maximum(i - 1, 0)
    n_pairs = n_far // 2

    m_ref[...] = jnp.full(m_ref.shape, NEG_INF, F32)
    l_ref[...] = jnp.zeros(l_ref.shape, F32)
    acc_ref[...] = jnp.zeros(acc_ref.shape, F32)

    def lane_tiles(x):
        return [x[:, n * BLOCK:(n + 1) * BLOCK] for n in range(x.shape[1] // BLOCK)]

    def stage_a(slot, j, kind, kill=None):
        start = pl.multiple_of(j * t, t)
        for c in range(2):
            qc = q_ref[:, c * HEAD_DIM:(c + 1) * HEAD_DIM]
            kc = k_ref[pl.ds(start, t), pl.ds(c * HEAD_DIM, HEAD_DIM)]
            s = lax.dot_general(qc, kc, (((1,), (1,)), ((), ())), preferred_element_type=F32)
            if kind == "prev":
                if kill is not None:
                    s = s + kill
                near = s[:BLOCK, t - BLOCK:] + bcorner_ref[...]
                top = jnp.concatenate([s[:BLOCK, :t - BLOCK], near], axis=1)
                s = jnp.concatenate([top, s[BLOCK:]], axis=0)
            elif kind == "diag":
                row = lax.broadcasted_iota(jnp.int32, (t, t), 0)
                col = lax.broadcasted_iota(jnp.int32, (t, t), 1)
                s = jnp.where(row >= col, s + bdiag_ref[...], NEG_INF)
            s_buf[slot, c] = s
            mx_buf[slot, c] = functools.reduce(jnp.maximum, lane_tiles(s))

    def stage_b(slot):
        for c in range(2):
            m_prev = m_ref[c]
            m_new = jnp.maximum(m_prev, jnp.max(mx_buf[slot, c], axis=1, keepdims=True))
            alpha = jnp.exp2(m_prev - m_new)
            p = jnp.exp2(s_buf[slot, c] - jnp.concatenate([m_new] * n_lane, axis=1))
            l_ref[c] = alpha * l_ref[c] + functools.reduce(jnp.add, lane_tiles(p))
            p_buf[slot, c] = p.astype(BF16)
            a_buf[slot, c] = alpha
            m_ref[c] = m_new

    def stage_c(slot, j):
        start = pl.multiple_of(j * t, t)
        vj = v_ref[pl.ds(start, t), :]
        for c in range(2):
            alpha = a_buf[slot, c]
            pv = jnp.dot(p_buf[slot, c], vj, preferred_element_type=F32)
            acc_ref[c] = jnp.concatenate([alpha] * (DIFF_V_DIM // BLOCK), axis=1) * acc_ref[c] + pv

    def pair(u, carry):
        t0 = 2 * u
        stage_c(0, jnp.maximum(t0 - 2, 0))
        stage_b(1)
        stage_a(0, t0, "far")
        stage_c(1, jnp.maximum(t0 - 1, 0))
        stage_b(0)
        stage_a(1, t0 + 1, "far")
        return carry

    @pl.when(n_pairs >= 1)
    def _():
        stage_a(0, 0, "far")
        stage_b(0)
        stage_a(1, 1, "far")

    @pl.when(n_pairs == 0)
    def _():
        s_buf[1] = jnp.full(s_buf.shape[1:], -jnp.inf, F32)
        mx_buf[1] = jnp.full(mx_buf.shape[1:], -jnp.inf, F32)
        p_buf[0] = jnp.zeros(p_buf.shape[1:], BF16)
        a_buf[0] = jnp.ones(a_buf.shape[1:], F32)

    n_quads = jnp.maximum(n_pairs - 1, 0) // 2

    def quad(u, carry):
        pair(1 + 2 * u, carry)
        return pair(2 + 2 * u, carry)

    lax.fori_loop(0, n_quads, quad, 0)
    lax.fori_loop(jnp.minimum(n_pairs, 1) + 2 * n_quads, n_pairs, pair, 0)

    e = 2 * n_pairs
    prev = jnp.maximum(i - 1, 0)
    odd = n_far % 2 == 1

    @pl.when(odd)
    def _():
        stage_c(0, jnp.maximum(e - 2, 0))
        stage_b(1)
        stage_a(0, e, "far")
        stage_c(1, jnp.maximum(e - 1, 0))
        stage_b(0)
        stage_a(1, prev, "prev")
        stage_c(0, e)
        stage_b(1)
        stage_a(0, i, "diag")
        stage_c(1, prev)
        stage_b(0)
        stage_c(0, i)

    @pl.when(jnp.logical_not(odd))
    def _():
        kill = jnp.where(i >= 1, 0.0, -jnp.inf).astype(F32)
        stage_c(0, jnp.maximum(e - 2, 0))
        stage_b(1)
        stage_a(0, prev, "prev", kill)
        stage_c(1, jnp.maximum(e - 1, 0))
        stage_b(0)
        stage_a(1, i, "diag")
        stage_c(0, prev)
        stage_b(1)
        stage_c(1, i)

    lam = (jnp.exp(jnp.sum(lq1_ref[...] * lk1_ref[...], axis=1, keepdims=True))
           - jnp.exp(jnp.sum(lq2_ref[...] * lk2_ref[...], axis=1, keepdims=True)) + lam_init)
    l1 = jnp.sum(l_ref[0], axis=1, keepdims=True)
    l2 = jnp.sum(l_ref[1], axis=1, keepdims=True)
    out = acc_ref[0] / l1 - lam * (acc_ref[1] / l2)
    y = out * lax.rsqrt(jnp.mean(out * out, axis=-1, keepdims=True) + EPS)
    o_ref[...] = ((y * g_ref[...]) * (1.0 - lam_init)).astype(o_ref.dtype)


def _diff_attn(proj, bias_diag, bias_corner, lq1, lk1, lq2, lk2, subln_g, lam_init):
    t = DIFF_T
    w = 2 * HEAD_DIM
    q_col, k_col, v_col = QB_OFF // w, KB_OFF // w, VB_OFF // w
    vec = pl.BlockSpec((1, HEAD_DIM), lambda h, i: (0, 0))
    return pl.pallas_call(
        functools.partial(_diff_kernel, lam_init=lam_init),
        grid=(DIFF_HEADS, SEQ // t),
        in_specs=[
            pl.BlockSpec((t, w), lambda h, i: (i, q_col + h)),
            pl.BlockSpec((SEQ, w), lambda h, i: (0, k_col + h)),
            pl.BlockSpec((SEQ, w), lambda h, i: (0, v_col + h)),
            pl.BlockSpec((None, t, t), lambda h, i: (h, 0, 0)),
            pl.BlockSpec((None, BLOCK, BLOCK), lambda h, i: (h, 0, 0)),
            vec, vec, vec, vec,
            pl.BlockSpec((1, DIFF_V_DIM), lambda h, i: (0, 0)),
        ],
        out_specs=pl.BlockSpec((t, DIFF_V_DIM), lambda h, i: (i, h)),
        out_shape=jax.ShapeDtypeStruct((SEQ, VB_COLS), BF16),
        scratch_shapes=[
            pltpu.VMEM((2, t, BLOCK), F32),
            pltpu.VMEM((2, t, BLOCK), F32),
            pltpu.VMEM((2, t, DIFF_V_DIM), F32),
            pltpu.VMEM((2, 2, t, t), F32),
            pltpu.VMEM((2, 2, t, BLOCK), F32),
            pltpu.VMEM((2, 2, t, t), BF16),
            pltpu.VMEM((2, 2, t, BLOCK), F32),
        ],
        compiler_params=_params(2),
        name="diff_attn",
    )(proj, proj, proj, bias_diag, bias_corner, lq1, lk1, lq2, lk2, subln_g)


def _out_kernel(ya_ref, yb_ref, w_ref, x_ref, g_ref, o_ref, wb_ref):
    _cast_weight_once(w_ref, wb_ref)
    acc = jnp.dot(ya_ref[...], wb_ref[:QA_COLS, :], preferred_element_type=F32)
    acc = acc + jnp.dot(yb_ref[...], wb_ref[QA_COLS:, :], preferred_element_type=F32)
    o_ref[...] = x_ref[...] + g_ref[...] * acc


def _out_proj(ya, yb, w_all, layer, x, gate):
    return pl.pallas_call(
        _out_kernel,
        grid=(D_MODEL // OUT_TN, SEQ // OUT_TM),
        in_specs=[
            pl.BlockSpec((OUT_TM, QA_COLS), lambda j, i: (i, 0)),
            pl.BlockSpec((OUT_TM, VB_COLS), lambda j, i: (i, 0)),
            pl.BlockSpec((None, QA_COLS + VB_COLS, OUT_TN), lambda j, i: (layer, 0, j)),
            pl.BlockSpec((OUT_TM, OUT_TN), lambda j, i: (i, j)),
            pl.BlockSpec((1, OUT_TN), lambda j, i: (0, j)),
        ],
        out_specs=pl.BlockSpec((OUT_TM, OUT_TN), lambda j, i: (i, j)),
        out_shape=jax.ShapeDtypeStruct((SEQ, D_MODEL), F32),
        scratch_shapes=[pltpu.VMEM((QA_COLS + VB_COLS, OUT_TN), BF16)],
        compiler_params=_params(2),
        name="out_proj",
    )(ya, yb, w_all, x, gate)


def _up_kernel(h_ref, wg_ref, wu_ref, cw_ref, cb_ref, o_ref, carry_ref, wb_ref, raw_ref, *, n_rows):
    s = pl.program_id(0)
    last = pl.num_programs(0) - 2
    row_mm = jnp.minimum(s, last) % n_rows
    row_ep = jnp.maximum(s - 1, 0) % n_rows

    @pl.when(s == 0)
    def _():
        raw_ref[...] = jnp.zeros(raw_ref.shape, F32)

    @pl.when(row_mm == 0)
    def _():
        wb_ref[:, :UP_TN] = wg_ref[...].astype(BF16)
        wb_ref[:, UP_TN:] = wu_ref[...].astype(BF16)

    @pl.when(row_ep == 0)
    def _():
        carry_ref[...] = jnp.zeros(carry_ref.shape, F32)

    prev = carry_ref[...]
    row = lax.broadcasted_iota(jnp.int32, (CARRY_ROWS, UP_TN), 0)
    gate = raw_ref[:, :UP_TN]
    up = raw_ref[:, UP_TN:]
    conv = cb_ref[...] + gate * cw_ref[CONV_WIDTH - 1:CONV_WIDTH, :]
    for d in range(1, CONV_WIDTH):
        shifted = pltpu.roll(gate, d, 0)
        head = jnp.where(row < d, pltpu.roll(prev, d, 0), shifted[:CARRY_ROWS])
        shifted = jnp.concatenate([head, shifted[CARRY_ROWS:]], axis=0)
        conv = conv + shifted * cw_ref[CONV_WIDTH - 1 - d:CONV_WIDTH - d, :]
    act = conv / (1.0 + jnp.exp(-conv)) * up
    o_ref[...] = act.astype(o_ref.dtype)
    carry_ref[...] = gate[UP_TM - CARRY_ROWS:]

    for r in _row_chunks(UP_TM):
        raw_ref[r, :] = jnp.dot(h_ref[r, :], wb_ref[...], preferred_element_type=F32)


def _mlp_up(h, w_all, layer, conv_w, conv_b):
    n_cols = D_FF // UP_TN
    n_rows = SEQ // UP_TM
    last = n_cols * n_rows - 1

    def mm_row(s):
        return jnp.minimum(s, last) % n_rows

    def mm_col(s):
        return jnp.minimum(s, last) // n_rows

    def ep_row(s):
        return jnp.maximum(s - 1, 0) % n_rows

    def ep_col(s):
        return jnp.maximum(s - 1, 0) // n_rows

    return pl.pallas_call(
        functools.partial(_up_kernel, n_rows=n_rows),
        grid=(last + 2,),
        in_specs=[
            pl.BlockSpec((UP_TM, D_MODEL), lambda s: (mm_row(s), 0)),
            pl.BlockSpec((None, D_MODEL, UP_TN), lambda s: (layer, 0, mm_col(s))),
            pl.BlockSpec((None, D_MODEL, UP_TN), lambda s: (layer, 0, n_cols + mm_col(s))),
            pl.BlockSpec((CONV_WIDTH, UP_TN), lambda s: (0, ep_col(s))),
            pl.BlockSpec((1, UP_TN), lambda s: (0, ep_col(s))),
        ],
        out_specs=pl.BlockSpec((UP_TM, UP_TN), lambda s: (ep_row(s), ep_col(s))),
        out_shape=jax.ShapeDtypeStruct((SEQ, D_FF), BF16),
        scratch_shapes=[pltpu.VMEM((CARRY_ROWS, UP_TN), F32), pltpu.VMEM((D_MODEL, 2 * UP_TN), BF16),
                        pltpu.VMEM((UP_TM, 2 * UP_TN), F32)],
        compiler_params=_params(1),
        name="mlp_up",
    )(h, w_all, w_all, conv_w, conv_b)


def _cast_kernel(w_ref, o_ref):
    o_ref[...] = w_ref[...].astype(o_ref.dtype)


def _down_weight_bf16(w_all, layer):
    rows = D_FF // CAST_STEPS
    return pl.pallas_call(
        _cast_kernel,
        grid=(CAST_STEPS,),
        in_specs=[pl.BlockSpec((None, rows, D_MODEL), lambda i: (layer, i, 0))],
        out_specs=pl.BlockSpec((rows, D_MODEL), lambda i: (i, 0)),
        out_shape=jax.ShapeDtypeStruct((D_FF, D_MODEL), BF16),
        compiler_params=_params(1),
        name="cast_w_down",
    )(w_all)


def _down_kernel(a_ref, w_ref, x_ref, g_ref, o_ref):
    acc = jnp.dot(a_ref[...], w_ref[...], preferred_element_type=F32)
    o_ref[...] = x_ref[...] + g_ref[...] * acc


def _mlp_down(act, w, x, gate):
    return pl.pallas_call(
        _down_kernel,
        grid=(D_MODEL // DOWN_TN, SEQ // DOWN_TM),
        in_specs=[
            pl.BlockSpec((DOWN_TM, D_FF), lambda j, i: (i, 0)),
            pl.BlockSpec((D_FF, DOWN_TN), lambda j, i: (0, j)),
            pl.BlockSpec((DOWN_TM, DOWN_TN), lambda j, i: (i, j)),
            pl.BlockSpec((1, DOWN_TN), lambda j, i: (0, j)),
        ],
        out_specs=pl.BlockSpec((DOWN_TM, DOWN_TN), lambda j, i: (i, j)),
        out_shape=jax.ShapeDtypeStruct((SEQ, D_MODEL), F32),
        compiler_params=_params(2),
        name="mlp_down",
    )(act, w, x, gate)


def _t5_bucket(n):
    n = jnp.maximum(n, 0)
    nf = jnp.maximum(n, 1).astype(F32)
    large = MAX_EXACT + (jnp.log(nf / MAX_EXACT) / math.log(MAX_DISTANCE / MAX_EXACT)
                         * (NUM_BUCKETS - MAX_EXACT)).astype(jnp.int32)
    large = jnp.minimum(large, NUM_BUCKETS - 1)
    return jnp.where(n < MAX_EXACT, n, large)


def _toeplitz_tiles(rel):
    n_heads = rel.shape[1]
    period = 3 * BLOCK
    u = np.arange(period)
    by_dist = rel[_t5_bucket(jnp.asarray(u - (BLOCK - 1)))]
    w = by_dist[(2 * BLOCK - 1 - u) % period].T
    skew = jnp.tile(w, (1, BLOCK))[:, :BLOCK * (period - 1)].reshape(n_heads, BLOCK, period - 1)
    return skew[:, :, :2 * BLOCK]


def _bias_tables(rel_bias):
    tiles = _toeplitz_tiles(rel_bias.astype(F32))
    bias_a = tiles[:SWA_Q_HEADS]
    far = rel_bias[NUM_BUCKETS - 1, SWA_Q_HEADS:].astype(F32)
    tb = (tiles[SWA_Q_HEADS:] - far[:, None, None]) * LOG2E
    sub_diag, on_diag = tb[:, :, :BLOCK], tb[:, :, BLOCK:]
    n = DIFF_T // BLOCK
    zero = jnp.zeros_like(on_diag)
    rows = [jnp.concatenate([on_diag if r == c else sub_diag if r == c + 1 else zero for c in range(n)], axis=2)
            for r in range(n)]
    bias_diag = jnp.concatenate(rows, axis=1)
    return bias_a, bias_diag, sub_diag


def _col_scale():
    s = np.ones((1, IN_COLS), np.float32)
    s[:, :QA_COLS] = HEAD_DIM ** -0.5
    s[:, QB_OFF:QB_OFF + QB_COLS] = HEAD_DIM ** -0.5 * LOG2E
    return jnp.asarray(s)


def kernel(x, c, ada_w, ada_b, attn_norm_g, mlp_norm_g, w_in, swa_sinks, diff_lq1, diff_lk1,
           diff_lq2, diff_lk2, diff_subln_g, w_out, rel_bias, w_up, conv_w, conv_b, w_down, final_g):
    assert x.shape == (1, SEQ, D_MODEL) and c.shape == (1, D_MODEL)
    assert DIFF_T >= 2 * BLOCK and BLOCK >= MAX_DISTANCE
    xs = x.reshape(SEQ, D_MODEL)
    mod = _ada_mod(c, ada_w, ada_b)
    bias_a, bias_diag, bias_corner = _bias_tables(rel_bias)
    col_scale = _col_scale()
    for l in range(DEPTH):
        sh_a, sc_a, g_a, sh_m, sc_m, g_m = [mod[l, :, k * D_MODEL:(k + 1) * D_MODEL] for k in range(N_MOD)]
        lam_init = 0.8 - 0.6 * math.exp(-0.3 * l)
        h = _modulate(xs, attn_norm_g[l][None], sc_a, sh_a)
        proj = _in_proj(h, w_in, l, col_scale)
        ya = _swa(proj, swa_sinks[l], bias_a)
        yb = _diff_attn(proj, bias_diag, bias_corner, diff_lq1[l][None], diff_lk1[l][None],
                        diff_lq2[l][None], diff_lk2[l][None], diff_subln_g[l][None], lam_init)
        xs = _out_proj(ya, yb, w_out, l, xs, g_a)
        h = _modulate(xs, mlp_norm_g[l][None], sc_m, sh_m)
        act = _mlp_up(h, w_up, l, conv_w[l], conv_b[l][None])
        xs = _mlp_down(act, _down_weight_bf16(w_down, l), xs, g_m)
    return _final_norm(xs, final_g[None]).reshape(1, SEQ, D_MODEL)
```
